```python
import math
import jax, jax.numpy as jnp
from jax import lax
import numpy as np

D_MODEL = 1024
BATCH = 4
SEQ = 8192
DEPTH = 1
DEC_BATCH = 8
DEC_SEQ = 4096
PAST_LEN = 128

D_FF = 2816
N_HEADS_A = 8
N_KV_A = 2
HEAD_DIM_A = 64
GROUP_A = N_HEADS_A // N_KV_A
WINDOW = 128
BLOCK = 128
N_HEADS_B = 8
Q_LORA = 256
KV_LORA = 128
NOPE_B = 64
ROPE_B = 32
QK_DIM_B = NOPE_B + ROPE_B
V_DIM_B = 64
ROPE_THETA = 10000.0
EPS = 1e-6
A_Q = N_HEADS_A * HEAD_DIM_A
A_KV = N_KV_A * HEAD_DIM_A
IN_WIDTH = A_Q + 2 * A_KV + Q_LORA + KV_LORA + ROPE_B
MIX_WIDTH = N_HEADS_A * HEAD_DIM_A + N_HEADS_B * V_DIM_B
N_MOD = 9

kernel_name = "hymba_swa_mla_macaron_adaln_encoder"


def rmsnorm(x, g):
    xf = x.astype(jnp.float32)
    y = xf * lax.rsqrt(jnp.mean(xf * xf, axis=-1, keepdims=True) + EPS)
    return (y * g.astype(jnp.float32)).astype(x.dtype)


def modulate(h, shift, scale):
    return h * (1 + scale[:, None, :]) + shift[:, None, :]


def rope(x):
    S, d = x.shape[1], x.shape[-1]
    pos = jnp.arange(S, dtype=jnp.float32)
    inv = ROPE_THETA ** (-jnp.arange(0, d, 2, dtype=jnp.float32) / d)
    ang = pos[:, None] * inv[None, :]
    cos = jnp.cos(ang)[None, :, None, :]
    sin = jnp.sin(ang)[None, :, None, :]
    xf = x.astype(jnp.float32)
    x1, x2 = xf[..., : d // 2], xf[..., d // 2:]
    out = jnp.concatenate([x1 * cos - x2 * sin, x2 * cos + x1 * sin], axis=-1)
    return out.astype(x.dtype)


def swiglu(h, wg, wu, wd):
    return (jax.nn.silu(h @ wg) * (h @ wu)) @ wd


def window_attention(q, k, v, sink):
    B, S, H, d = q.shape
    nb = S // BLOCK
    qb = q.reshape(B, nb, BLOCK, N_KV_A, GROUP_A, d)
    pad = ((0, 0), (BLOCK, BLOCK), (0, 0), (0, 0))
    kp = jnp.pad(k, pad).reshape(B, nb + 2, BLOCK, N_KV_A, d)
    vp = jnp.pad(v, pad).reshape(B, nb + 2, BLOCK, N_KV_A, d)
    kw = jnp.concatenate([kp[:, :-2], kp[:, 1:-1], kp[:, 2:]], axis=2)
    vw = jnp.concatenate([vp[:, :-2], vp[:, 1:-1], vp[:, 2:]], axis=2)
    s = jnp.einsum('bnqhgd,bnjhd->bnhgqj', qb, kw).astype(jnp.float32) / math.sqrt(d)
    qi = jnp.arange(BLOCK)[:, None]
    kj = jnp.arange(3 * BLOCK)[None, :]
    band = jnp.abs(kj - BLOCK - qi) <= WINDOW
    kpos = jnp.arange(nb)[:, None] * BLOCK + jnp.arange(3 * BLOCK)[None, :] - BLOCK
    inside = (kpos >= 0) & (kpos < S)
    mask = band[None, :, :] & inside[:, None, :]
    s = jnp.where(mask[None, :, None, None, :, :], s, -1e30)
    sk = sink.astype(jnp.float32).reshape(N_KV_A, GROUP_A)[None, None, :, :, None, None]
    m = jnp.maximum(jnp.max(s, axis=-1, keepdims=True), sk)
    p = jnp.exp(s - m)
    p = p / (jnp.sum(p, axis=-1, keepdims=True) + jnp.exp(sk - m))
    o = jnp.einsum('bnhgqj,bnjhd->bnqhgd', p.astype(v.dtype), vw)
    return o.reshape(B, S, H * d)


def dense_attention(q, k, v):
    B, S, H, dq = q.shape
    nb = S // BLOCK
    qb = q.reshape(B, nb, BLOCK, H, dq).transpose(1, 0, 2, 3, 4)
    scale = 1.0 / math.sqrt(dq)

    def one(qblk):
        s = jnp.einsum('bqhd,bkhd->bhqk', qblk, k).astype(jnp.float32) * scale
        p = jax.nn.softmax(s, axis=-1)
        return jnp.einsum('bhqk,bkhd->bqhd', p.astype(v.dtype), v)

    o = lax.map(one, qb)
    return o.transpose(1, 0, 2, 3, 4).reshape(B, S, H * v.shape[-1])


def token_mixing(h, w_in, swa_q_norm, swa_k_norm, swa_sink, mla_q_lora_norm, mla_w_uq,
                 mla_kv_lora_norm, mla_w_ukv, mla_q_norm, mla_k_norm, w_out):
    B, S, _ = h.shape
    z = h @ w_in
    cuts = np.cumsum([A_Q, A_KV, A_KV, Q_LORA, KV_LORA]).tolist()
    qa, ka, va, cq, ckv, kr = jnp.split(z, cuts, axis=-1)
    qa = rope(rmsnorm(qa.reshape(B, S, N_HEADS_A, HEAD_DIM_A), swa_q_norm))
    ka = rope(rmsnorm(ka.reshape(B, S, N_KV_A, HEAD_DIM_A), swa_k_norm))
    va = va.reshape(B, S, N_KV_A, HEAD_DIM_A)
    oa = window_attention(qa, ka, va, swa_sink)
    qb = (rmsnorm(cq, mla_q_lora_norm) @ mla_w_uq).reshape(B, S, N_HEADS_B, QK_DIM_B)
    kv = (rmsnorm(ckv, mla_kv_lora_norm) @ mla_w_ukv).reshape(B, S, N_HEADS_B, NOPE_B + V_DIM_B)
    k_nope, vb = kv[..., :NOPE_B], kv[..., NOPE_B:]
    k_rope = jnp.broadcast_to(kr[:, :, None, :], (B, S, N_HEADS_B, ROPE_B))
    kb = jnp.concatenate([k_nope, k_rope], axis=-1)
    qb = rmsnorm(qb, mla_q_norm)
    kb = rmsnorm(kb, mla_k_norm)
    qb = jnp.concatenate([qb[..., :NOPE_B], rope(qb[..., NOPE_B:])], axis=-1)
    kb = jnp.concatenate([kb[..., :NOPE_B], rope(kb[..., NOPE_B:])], axis=-1)
    ob = dense_attention(qb, kb, vb)
    return jnp.concatenate([oa, ob], axis=-1) @ w_out


def setup_inputs(seed: int = 0) -> dict:
    key = jax.random.key(seed)
    ks = jax.random.split(key, 32)
    f32 = jnp.float32

    def nrm(k, shape, std):
        return jax.random.normal(k, shape, f32) * std

    def gain(k, dim):
        return 1.0 + 0.02 * jax.random.normal(k, (DEPTH, dim), f32)

    L = DEPTH
    return {
        "x_prompt": nrm(ks[0], (BATCH, SEQ, D_MODEL), 1.0),
        "x_sample": nrm(ks[1], (DEC_BATCH, DEC_SEQ, D_MODEL), 1.0),
        "c_prompt": nrm(ks[2], (BATCH, D_MODEL), 1.0),
        "c_sample": nrm(ks[3], (DEC_BATCH, D_MODEL), 1.0),
        "ada_w": nrm(ks[4], (L, D_MODEL, N_MOD * D_MODEL), D_MODEL ** -0.5),
        "ada_b": nrm(ks[5], (L, N_MOD * D_MODEL), 0.02),
        "ffn1_norm": gain(ks[6], D_MODEL),
        "ffn1_wg": nrm(ks[7], (L, D_MODEL, D_FF), D_MODEL ** -0.5),
        "ffn1_wu": nrm(ks[8], (L, D_MODEL, D_FF), D_MODEL ** -0.5),
        "ffn1_wd": nrm(ks[9], (L, D_FF, D_MODEL), D_FF ** -0.5),
        "mix_norm": gain(ks[10], D_MODEL),
        "w_in": nrm(ks[11], (L, D_MODEL, IN_WIDTH), D_MODEL ** -0.5),
        "swa_q_norm": gain(ks[12], HEAD_DIM_A),
        "swa_k_norm": gain(ks[13], HEAD_DIM_A),
        "swa_sink": nrm(ks[14], (L, N_HEADS_A), 0.5),
        "mla_q_lora_norm": gain(ks[15], Q_LORA),
        "mla_w_uq": nrm(ks[16], (L, Q_LORA, N_HEADS_B * QK_DIM_B), Q_LORA ** -0.5),
        "mla_kv_lora_norm": gain(ks[17], KV_LORA),
        "mla_w_ukv": nrm(ks[18], (L, KV_LORA, N_HEADS_B * (NOPE_B + V_DIM_B)), KV_LORA ** -0.5),
        "mla_q_norm": gain(ks[19], QK_DIM_B),
        "mla_k_norm": gain(ks[20], QK_DIM_B),
        "w_out": nrm(ks[21], (L, MIX_WIDTH, D_MODEL), MIX_WIDTH ** -0.5),
        "ffn2_norm": gain(ks[22], D_MODEL),
        "ffn2_wg": nrm(ks[23], (L, D_MODEL, D_FF), D_MODEL ** -0.5),
        "ffn2_wu": nrm(ks[24], (L, D_MODEL, D_FF), D_MODEL ** -0.5),
        "ffn2_wd": nrm(ks[25], (L, D_FF, D_MODEL), D_FF ** -0.5),
        "final_norm": gain(ks[26], D_MODEL),
    }


def reference(x_prompt, x_sample, c_prompt, c_sample, ada_w, ada_b, ffn1_norm, ffn1_wg, ffn1_wu,
              ffn1_wd, mix_norm, w_in, swa_q_norm, swa_k_norm, swa_sink, mla_q_lora_norm, mla_w_uq,
              mla_kv_lora_norm, mla_w_ukv, mla_q_norm, mla_k_norm, w_out, ffn2_norm, ffn2_wg,
              ffn2_wu, ffn2_wd, final_norm):
    def trunk(x, c):
        for l in range(DEPTH):
            mod = jax.nn.silu(c) @ ada_w[l] + ada_b[l]
            sh1, sc1, g1, sh2, sc2, g2, sh3, sc3, g3 = jnp.split(mod, N_MOD, axis=-1)
            h = modulate(rmsnorm(x, ffn1_norm[l]), sh1, sc1)
            x = x + 0.5 * g1[:, None, :] * swiglu(h, ffn1_wg[l], ffn1_wu[l], ffn1_wd[l])
            h = modulate(rmsnorm(x, mix_norm[l]), sh2, sc2)
            y = token_mixing(h, w_in[l], swa_q_norm[l], swa_k_norm[l], swa_sink[l],
                             mla_q_lora_norm[l], mla_w_uq[l], mla_kv_lora_norm[l], mla_w_ukv[l],
                             mla_q_norm[l], mla_k_norm[l], w_out[l])
            x = x + g2[:, None, :] * y
            h = modulate(rmsnorm(x, ffn2_norm[l]), sh3, sc3)
            x = x + 0.5 * g3[:, None, :] * swiglu(h, ffn2_wg[l], ffn2_wu[l], ffn2_wd[l])
            x = rmsnorm(x, final_norm[l])
        return x

    y_prompt = trunk(x_prompt, c_prompt)
    y_sample = trunk(x_sample, c_sample)
    return (y_prompt, y_sample)
```

```python
import functools
import math

import jax
import jax.numpy as jnp
from jax import lax
from jax.experimental import pallas as pl
from jax.experimental.pallas import tpu as pltpu

F32 = jnp.float32
BF16 = jnp.bfloat16

D_MODEL = 1024
D_FF = 2816
N_HEADS_A = 8
N_KV_A = 2
HEAD_DIM_A = 64
GROUP_A = N_HEADS_A // N_KV_A
WINDOW = 128
N_HEADS_B = 8
Q_LORA = 256
KV_LORA = 128
NOPE_B = 64
ROPE_B = 32
QK_DIM_B = NOPE_B + ROPE_B
V_DIM_B = 64
ROPE_THETA = 10000.0
EPS = 1e-6
N_MOD = 9
A_Q = N_HEADS_A * HEAD_DIM_A
A_KV = N_KV_A * HEAD_DIM_A
IN_WIDTH = A_Q + 2 * A_KV + Q_LORA + KV_LORA + ROPE_B
NEG_BIG = -1e30

LANE = 128
QK_PAD_B = LANE
VMEM_LIMIT_BYTES = 56 * 1024 * 1024

MOD_ROWS = 16
MOD_BN = 1152
FFN_TM = 512
FFN_FC = 256
FFN_NC = D_FF // FFN_FC
PROJ_TM = 512
SWA_TQ = 512
MLA_TQ = 1024
MLA_TK = PROJ_TM

R_QA, R_KA, R_VA = 0, A_Q, A_Q + A_KV
R_CQ = A_Q + 2 * A_KV
R_CKV = R_CQ + Q_LORA
R_KR = R_CKV + KV_LORA
G_QA, G_KA = 0, HEAD_DIM_A
G_QL = 2 * HEAD_DIM_A
G_KVL = G_QL + Q_LORA
G_MQ = G_KVL + KV_LORA
G_KN = G_MQ + QK_PAD_B
G_KR = G_KN + NOPE_B
G_ROWS = G_KR + ROPE_B
T_CA, T_SA = 0, HEAD_DIM_A // 2
T_CB = HEAD_DIM_A
T_SB = T_CB + ROPE_B // 2
T_ROWS = T_SB + ROPE_B // 2

_NT = (((1,), (1,)), ((), ()))
_TN = (((0,), (0,)), ((), ()))


def _cparams(sem):
    return pltpu.CompilerParams(dimension_semantics=sem, vmem_limit_bytes=VMEM_LIMIT_BYTES)


def _const_spec(shape):
    n = len(shape)
    return pl.BlockSpec(shape, lambda *_: (0,) * n, pipeline_mode=pl.Buffered(1))


def _silu(x):
    return x * (1.0 / (1.0 + jnp.exp(-x)))


def _rms_rows(x):
    return lax.rsqrt(jnp.mean(x * x, axis=-1, keepdims=True) + EPS)


def _mod_kernel(c_ref, w_ref, b_ref, o_ref):
    a = _silu(c_ref[...]).astype(BF16)
    o_ref[...] = jnp.dot(a, w_ref[...].astype(BF16), preferred_element_type=F32) + b_ref[...]


def _modulation(c, ada_w, ada_b):
    n = ada_w.shape[1]
    return pl.pallas_call(
        _mod_kernel,
        grid=(n // MOD_BN,),
        in_specs=[
            pl.BlockSpec((MOD_ROWS, D_MODEL), lambda j: (0, 0)),
            pl.BlockSpec((D_MODEL, MOD_BN), lambda j: (0, j)),
            pl.BlockSpec((1, MOD_BN), lambda j: (0, j)),
        ],
        out_specs=pl.BlockSpec((MOD_ROWS, MOD_BN), lambda j: (0, j)),
        out_shape=jax.ShapeDtypeStruct((MOD_ROWS, n), F32),
        compiler_params=_cparams(("arbitrary",)),
        name="mod",
    )(c, ada_w, ada_b)


def _swiglu_into(acc_ref, h_ref, wg_ref, wu_ref, wd_ref):
    acc_ref[...] = jnp.zeros_like(acc_ref)

    def body(c, carry):
        h = h_ref[...]
        g = jnp.dot(h, wg_ref[c], preferred_element_type=F32)
        u = jnp.dot(h, wu_ref[c], preferred_element_type=F32)
        a = (_silu(g) * u).astype(BF16)
        acc_ref[...] += jnp.dot(a, wd_ref[c], preferred_element_type=F32)
        return carry

    lax.fori_loop(0, FFN_NC, body, 0)


def _ffn1_kernel(x_ref, sh_ref, sc_ref, gt_ref, nw_ref, wg_ref, wu_ref, wd_ref, o_ref, h_ref, acc_ref):
    x = x_ref[...]
    h = x * _rms_rows(x) * nw_ref[...]
    h_ref[...] = (h * (1.0 + sc_ref[...]) + sh_ref[...]).astype(BF16)
    _swiglu_into(acc_ref, h_ref, wg_ref, wu_ref, wd_ref)
    o_ref[...] = x_ref[...] + (0.5 * gt_ref[...]) * acc_ref[...]


def _ffn2_kernel(x_ref, oa_ref, ob_ref, woa_ref, wob_ref, g2_ref, sh_ref, sc_ref, gt_ref, nw_ref,
                 wg_ref, wu_ref, wd_ref, fw_ref, o_ref, h_ref, acc_ref, x2_ref):
    y = lax.dot_general(oa_ref[...], woa_ref[...], _TN, preferred_element_type=F32)
    y = y + lax.dot_general(ob_ref[...], wob_ref[...], _TN, preferred_element_type=F32)
    x = x_ref[...] + g2_ref[...] * y
    x2_ref[...] = x
    h = x * _rms_rows(x) * nw_ref[...]
    h_ref[...] = (h * (1.0 + sc_ref[...]) + sh_ref[...]).astype(BF16)
    _swiglu_into(acc_ref, h_ref, wg_ref, wu_ref, wd_ref)
    z = x2_ref[...] + (0.5 * gt_ref[...]) * acc_ref[...]
    o_ref[...] = z * _rms_rows(z) * fw_ref[...]


def _row_spec():
    return pl.BlockSpec((None, 1, D_MODEL), lambda b, t: (b, 0, 0))


def _ffn_weight_specs():
    return [
        _const_spec((FFN_NC, D_MODEL, FFN_FC)),
        _const_spec((FFN_NC, D_MODEL, FFN_FC)),
        _const_spec((FFN_NC, FFN_FC, D_MODEL)),
    ]


def _ffn1(x, sh, sc, gt, nw, wg, wu, wd):
    B, S, _ = x.shape
    tm = FFN_TM
    tok = pl.BlockSpec((None, tm, D_MODEL), lambda b, t: (b, t, 0))
    return pl.pallas_call(
        _ffn1_kernel,
        grid=(B, S // tm),
        in_specs=[tok, _row_spec(), _row_spec(), _row_spec(), _const_spec((1, D_MODEL))] + _ffn_weight_specs(),
        out_specs=tok,
        out_shape=jax.ShapeDtypeStruct(x.shape, F32),
        scratch_shapes=[pltpu.VMEM((tm, D_MODEL), BF16), pltpu.VMEM((tm, D_MODEL), F32)],
        compiler_params=_cparams(("arbitrary", "arbitrary")),
        name="ffn1",
    )(x, sh, sc, gt, nw, wg, wu, wd)


def _ffn2(x, oa_t, ob_t, woa, wob, g2, sh, sc, gt, nw, wg, wu, wd, fw):
    B, S, _ = x.shape
    tm = FFN_TM
    tok = pl.BlockSpec((None, tm, D_MODEL), lambda b, t: (b, t, 0))
    att = pl.BlockSpec((None, A_Q, tm), lambda b, t: (b, 0, t))
    return pl.pallas_call(
        _ffn2_kernel,
        grid=(B, S // tm),
        in_specs=[tok, att, att, _const_spec((A_Q, D_MODEL)), _const_spec((A_Q, D_MODEL)),
                  _row_spec(), _row_spec(), _row_spec(), _row_spec(), _const_spec((1, D_MODEL))]
                 + _ffn_weight_specs() + [_const_spec((1, D_MODEL))],
        out_specs=tok,
        out_shape=jax.ShapeDtypeStruct(x.shape, F32),
        scratch_shapes=[pltpu.VMEM((tm, D_MODEL), BF16), pltpu.VMEM((tm, D_MODEL), F32),
                        pltpu.VMEM((tm, D_MODEL), F32)],
        compiler_params=_cparams(("arbitrary", "arbitrary")),
        name="ffn2",
    )(x, oa_t, ob_t, woa, wob, g2, sh, sc, gt, nw, wg, wu, wd, fw)


def _rms_cols(x, n):
    return lax.rsqrt(jnp.sum(x * x, axis=0, keepdims=True) * (1.0 / n) + EPS)


def _rope_cols(y, cos, sin):
    half = y.shape[0] // 2
    y1, y2 = y[:half], y[half:]
    return jnp.concatenate([y1 * cos - y2 * sin, y2 * cos + y1 * sin], axis=0)


def _inproj_kernel(x_ref, sh_ref, sc_ref, nw_ref, w1_ref, wuq_ref, wukv_ref, g_ref, t_ref,
                   qa_ref, ka_ref, va_ref, qb_ref, kb_ref, vb_ref):
    tm = x_ref.shape[0]
    x = x_ref[...]
    h = x * _rms_rows(x) * nw_ref[...]
    h = (h * (1.0 + sc_ref[...]) + sh_ref[...]).astype(BF16)
    zt = lax.dot_general(w1_ref[...], h, _NT, preferred_element_type=F32)

    cos_a, sin_a = t_ref[T_CA:T_CA + 32, :], t_ref[T_SA:T_SA + 32, :]
    cos_b, sin_b = t_ref[T_CB:T_CB + 16, :], t_ref[T_SB:T_SB + 16, :]

    g_qa = g_ref[G_QA:G_QA + HEAD_DIM_A, :]
    for hd in range(N_HEADS_A):
        blk = zt[R_QA + hd * HEAD_DIM_A:R_QA + (hd + 1) * HEAD_DIM_A]
        y = blk * g_qa * _rms_cols(blk, HEAD_DIM_A)
        qa_ref[hd] = (_rope_cols(y, cos_a, sin_a) * (1.0 / math.sqrt(HEAD_DIM_A))).astype(BF16)
    g_ka = g_ref[G_KA:G_KA + HEAD_DIM_A, :]
    ks = []
    for hd in range(N_KV_A):
        blk = zt[R_KA + hd * HEAD_DIM_A:R_KA + (hd + 1) * HEAD_DIM_A]
        ks.append(_rope_cols(blk * g_ka * _rms_cols(blk, HEAD_DIM_A), cos_a, sin_a))
    ka_ref[...] = jnp.concatenate(ks, axis=0).T.astype(BF16)
    vt = zt[R_VA:R_VA + A_KV]
    for c in range(tm // LANE):
        va_ref[c] = vt[:, c * LANE:(c + 1) * LANE].astype(BF16)

    cq = zt[R_CQ:R_CQ + Q_LORA]
    cqn = (cq * g_ref[G_QL:G_QL + Q_LORA, :] * _rms_cols(cq, Q_LORA)).astype(BF16)
    qb_all = jnp.dot(wuq_ref[...], cqn, preferred_element_type=F32)
    g_mq = g_ref[G_MQ:G_MQ + QK_PAD_B, :]
    pad = jnp.zeros((QK_PAD_B - QK_DIM_B, tm), F32)
    for hd in range(N_HEADS_B):
        blk = qb_all[hd * QK_PAD_B:(hd + 1) * QK_PAD_B]
        y = blk * g_mq
        o = jnp.concatenate([y[:NOPE_B], _rope_cols(y[NOPE_B:QK_DIM_B], cos_b, sin_b), pad], axis=0)
        qb_ref[hd] = (o * (_rms_cols(blk, QK_DIM_B) * (1.0 / math.sqrt(QK_DIM_B)))).astype(BF16)

    ckv = zt[R_CKV:R_CKV + KV_LORA]
    ckvn = (ckv * g_ref[G_KVL:G_KVL + KV_LORA, :] * _rms_cols(ckv, KV_LORA)).astype(BF16)
    kvt = jnp.dot(wukv_ref[...], ckvn, preferred_element_type=F32)
    kr = zt[R_KR:R_KR + ROPE_B]
    ss_kr = jnp.sum(kr * kr, axis=0, keepdims=True)
    kr_rot = _rope_cols(kr * g_ref[G_KR:G_KR + ROPE_B, :], cos_b, sin_b)
    g_kn = g_ref[G_KN:G_KN + NOPE_B, :]
    for hd in range(N_HEADS_B):
        kn = kvt[hd * 2 * NOPE_B:hd * 2 * NOPE_B + NOPE_B]
        r = lax.rsqrt((jnp.sum(kn * kn, axis=0, keepdims=True) + ss_kr) * (1.0 / QK_DIM_B) + EPS)
        kt = jnp.concatenate([kn * g_kn, kr_rot, pad], axis=0) * r
        kb_ref[hd] = kt.T.astype(BF16)
        vb_ref[hd] = kvt[hd * 2 * NOPE_B + NOPE_B:(hd + 1) * 2 * NOPE_B].astype(BF16)


def _inproj(x, sh, sc, nw, w1t, wuqt, wukvt, gains, table):
    B, S, _ = x.shape
    tm = PROJ_TM
    nt = S // tm
    out_shape = [
        jax.ShapeDtypeStruct((B, N_HEADS_A, HEAD_DIM_A, S), BF16),
        jax.ShapeDtypeStruct((B, S, A_KV), BF16),
        jax.ShapeDtypeStruct((B, S // LANE, A_KV, LANE), BF16),
        jax.ShapeDtypeStruct((B, N_HEADS_B, QK_PAD_B, S), BF16),
        jax.ShapeDtypeStruct((B, N_HEADS_B, S, QK_PAD_B), BF16),
        jax.ShapeDtypeStruct((B, N_HEADS_B, nt, V_DIM_B, tm), BF16),
    ]
    out_specs = [
        pl.BlockSpec((None, N_HEADS_A, HEAD_DIM_A, tm), lambda b, t: (b, 0, 0, t)),
        pl.BlockSpec((None, tm, A_KV), lambda b, t: (b, t, 0)),
        pl.BlockSpec((None, tm // LANE, A_KV, LANE), lambda b, t: (b, t, 0, 0)),
        pl.BlockSpec((None, N_HEADS_B, QK_PAD_B, tm), lambda b, t: (b, 0, 0, t)),
        pl.BlockSpec((None, N_HEADS_B, tm, QK_PAD_B), lambda b, t: (b, 0, t, 0)),
        pl.BlockSpec((None, N_HEADS_B, None, V_DIM_B, tm), lambda b, t: (b, 0, t, 0, 0)),
    ]
    return pl.pallas_call(
        _inproj_kernel,
        grid=(B, nt),
        in_specs=[
            pl.BlockSpec((None, tm, D_MODEL), lambda b, t: (b, t, 0)),
            _row_spec(), _row_spec(), _const_spec((1, D_MODEL)),
            _const_spec((IN_WIDTH, D_MODEL)),
            _const_spec((N_HEADS_B * QK_PAD_B, Q_LORA)),
            _const_spec((N_HEADS_B * 2 * NOPE_B, KV_LORA)),
            _const_spec((G_ROWS, tm)),
            pl.BlockSpec((T_ROWS, tm), lambda b, t: (0, t)),
        ],
        out_specs=out_specs,
        out_shape=out_shape,
        compiler_params=_cparams(("arbitrary", "arbitrary")),
        name="inproj",
    )(x, sh, sc, nw, w1t, wuqt, wukvt, gains, table)


def _swa_kernel(q_ref, k_ref, v_ref, sink_ref, o_ref, *, n_blocks):
    tq = q_ref.shape[-1]
    nq = tq // LANE
    wide = GROUP_A * LANE
    row = lax.broadcasted_iota(jnp.int32, (LANE, wide), 0)
    col = lax.broadcasted_iota(jnp.int32, (LANE, wide), 1) & (LANE - 1)
    diff = row - col
    zeros = jnp.zeros((HEAD_DIM_A, LANE), BF16)
    for i in range(nq):
        nb = pl.program_id(1) * nq + i
        kb_idx = [jnp.maximum(nb - 1, 0), nb, jnp.minimum(nb + 1, n_blocks - 1)]
        lo = jnp.where(nb > 0, 0, 2 * LANE)
        hi = jnp.where(nb < n_blocks - 1, 0, -2 * LANE)
        masks = [diff >= lo, None, diff <= hi]
        for j in range(N_KV_A):
            cols = []
            for g in range(GROUP_A):
                qh = q_ref[j * GROUP_A + g, :, i * LANE:(i + 1) * LANE]
                cols.append(jnp.concatenate([qh, zeros] if j == 0 else [zeros, qh], axis=0))
            qp = jnp.concatenate(cols, axis=1)
            sink = sink_ref[j]
            ss = []
            m = sink
            for c in range(3):
                k = k_ref[pl.ds(pl.multiple_of(kb_idx[c] * LANE, LANE), LANE), :]
                s = jnp.dot(k, qp, preferred_element_type=F32)
                if masks[c] is not None:
                    s = jnp.where(masks[c], s, NEG_BIG)
                ss.append(s)
                m = jnp.maximum(m, jnp.max(s, axis=0, keepdims=True))
            l = jnp.exp(sink - m)
            acc = jnp.zeros((HEAD_DIM_A, wide), F32)
            for c in range(3):
                p = jnp.exp(ss[c] - m)
                l = l + jnp.sum(p, axis=0, keepdims=True)
                vt = v_ref[kb_idx[c], j * HEAD_DIM_A:(j + 1) * HEAD_DIM_A, :]
                acc = acc + jnp.dot(vt, p.astype(BF16), preferred_element_type=F32)
            o = acc * (1.0 / l)
            for g in range(GROUP_A):
                o_ref[j * GROUP_A + g, :, i * LANE:(i + 1) * LANE] = o[:, g * LANE:(g + 1) * LANE].astype(BF16)


def _swa(qa_t, ka, va_t, sink_rows):
    B, _, _, S = qa_t.shape
    tq = SWA_TQ
    nbk = S // LANE
    qspec = pl.BlockSpec((None, N_HEADS_A, HEAD_DIM_A, tq), lambda b, t: (b, 0, 0, t))
    return pl.pallas_call(
        functools.partial(_swa_kernel, n_blocks=nbk),
        grid=(B, S // tq),
        in_specs=[
            qspec,
            pl.BlockSpec((None, S, A_KV), lambda b, t: (b, 0, 0)),
            pl.BlockSpec((None, nbk, A_KV, LANE), lambda b, t: (b, 0, 0, 0)),
            _const_spec((N_KV_A, 1, GROUP_A * LANE)),
        ],
        out_specs=qspec,
        out_shape=jax.ShapeDtypeStruct(qa_t.shape, BF16),
        compiler_params=_cparams(("arbitrary", "arbitrary")),
        name="swa",
    )(qa_t, ka, va_t, sink_rows)


def _mla_kernel(q_ref, k_ref, v_ref, o_ref, m_ref, l_ref, acc_ref):
    n_chunks = v_ref.shape[0]
    tk = v_ref.shape[-1]
    m_ref[...] = jnp.full_like(m_ref, NEG_BIG)
    l_ref[...] = jnp.zeros_like(l_ref)
    acc_ref[...] = jnp.zeros_like(acc_ref)

    def body(c, carry):
        k = k_ref[pl.ds(pl.multiple_of(c * tk, tk), tk), :]
        s = jnp.dot(k, q_ref[...], preferred_element_type=F32)
        m_old = m_ref[...]
        m_new = jnp.maximum(m_old, jnp.max(s, axis=0, keepdims=True))
        alpha = jnp.exp(m_old - m_new)
        p = jnp.exp(s - m_new)
        l_ref[...] = alpha * l_ref[...] + jnp.sum(p, axis=0, keepdims=True)
        acc_ref[...] = alpha * acc_ref[...] + jnp.dot(v_ref[c], p.astype(BF16), preferred_element_type=F32)
        m_ref[...] = m_new
        return carry

    lax.fori_loop(0, n_chunks, body, 0)
    o_ref[...] = (acc_ref[...] * (1.0 / l_ref[...])).astype(BF16)


def _mla(qb_t, kb, vb_t):
    B, H, _, S = qb_t.shape
    tq = MLA_TQ
    nck, tk = vb_t.shape[2], vb_t.shape[4]
    return pl.pallas_call(
        _mla_kernel,
        grid=(B, H, S // tq),
        in_specs=[
            pl.BlockSpec((None, None, QK_PAD_B, tq), lambda b, h, t: (b, h, 0, t)),
            pl.BlockSpec((None, None, S, QK_PAD_B), lambda b, h, t: (b, h, 0, 0)),
            pl.BlockSpec((None, None, nck, V_DIM_B, tk), lambda b, h, t: (b, h, 0, 0, 0)),
        ],
        out_specs=pl.BlockSpec((None, None, V_DIM_B, tq), lambda b, h, t: (b, h, 0, t)),
        out_shape=jax.ShapeDtypeStruct((B, H, V_DIM_B, S), BF16),
        scratch_shapes=[pltpu.VMEM((1, tq), F32), pltpu.VMEM((1, tq), F32), pltpu.VMEM((V_DIM_B, tq), F32)],
        compiler_params=_cparams(("arbitrary", "arbitrary", "arbitrary")),
        name="mla",
    )(qb_t, kb, vb_t)


def _rope_table(S):
    pos = jnp.arange(S, dtype=F32)

    def cs(d):
        inv = ROPE_THETA ** (-jnp.arange(0, d, 2, dtype=F32) / d)
        ang = pos[:, None] * inv[None, :]
        return jnp.cos(ang).T, jnp.sin(ang).T

    ca, sa = cs(HEAD_DIM_A)
    cb, sb = cs(ROPE_B)
    return jnp.concatenate([ca, sa, cb, sb], axis=0)


def _gain_tile(swa_q_norm, swa_k_norm, mla_q_lora_norm, mla_kv_lora_norm, mla_q_norm, mla_k_norm):
    rows = jnp.concatenate([
        swa_q_norm, swa_k_norm, mla_q_lora_norm, mla_kv_lora_norm,
        mla_q_norm, jnp.zeros((QK_PAD_B - QK_DIM_B,), F32),
        mla_k_norm[:NOPE_B], mla_k_norm[NOPE_B:],
    ]).astype(F32)
    return jnp.broadcast_to(rows[:, None], (G_ROWS, PROJ_TM))


def _chunk_cols(w):
    d, f = w.shape
    return w.reshape(d, f // FFN_FC, FFN_FC).transpose(1, 0, 2).astype(BF16)


def _chunk_rows(w):
    f, d = w.shape
    return w.reshape(f // FFN_FC, FFN_FC, d).astype(BF16)


def kernel(x_prompt, x_sample, c_prompt, c_sample, ada_w, ada_b, ffn1_norm, ffn1_wg, ffn1_wu, ffn1_wd, mix_norm, w_in, swa_q_norm, swa_k_norm, swa_sink, mla_q_lora_norm, mla_w_uq, mla_kv_lora_norm, mla_w_ukv, mla_q_norm, mla_k_norm, w_out, ffn2_norm, ffn2_wg, ffn2_wu, ffn2_wd, final_norm):
    assert ada_w.shape[0] == 1, "single-layer trunk"
    bp, bs = x_prompt.shape[0], x_sample.shape[0]
    assert bp + bs <= MOD_ROWS

    f1 = (_chunk_cols(ffn1_wg[0]), _chunk_cols(ffn1_wu[0]), _chunk_rows(ffn1_wd[0]))
    f2 = (_chunk_cols(ffn2_wg[0]), _chunk_cols(ffn2_wu[0]), _chunk_rows(ffn2_wd[0]))
    w1t = w_in[0].T.astype(BF16)
    wuq = mla_w_uq[0].reshape(Q_LORA, N_HEADS_B, QK_DIM_B)
    wuq = jnp.pad(wuq, ((0, 0), (0, 0), (0, QK_PAD_B - QK_DIM_B))).reshape(Q_LORA, N_HEADS_B * QK_PAD_B)
    wuqt = wuq.T.astype(BF16)
    wukvt = mla_w_ukv[0].T.astype(BF16)
    woa = w_out[0, :A_Q].astype(BF16)
    wob = w_out[0, A_Q:].astype(BF16)
    gains = _gain_tile(swa_q_norm[0], swa_k_norm[0], mla_q_lora_norm[0], mla_kv_lora_norm[0],
                       mla_q_norm[0], mla_k_norm[0])
    sink_rows = jnp.repeat(swa_sink[0].astype(F32), LANE).reshape(N_KV_A, 1, GROUP_A * LANE)
    row = lambda v: v.reshape(1, D_MODEL)

    c_all = jnp.concatenate([c_prompt, c_sample, jnp.zeros((MOD_ROWS - bp - bs, D_MODEL), F32)], axis=0)
    mod = _modulation(c_all, ada_w[0], ada_b[0].reshape(1, -1)).reshape(MOD_ROWS, N_MOD, 1, D_MODEL)

    def trunk(x, m):
        sh1, sc1, g1, sh2, sc2, g2, sh3, sc3, g3 = (m[:, i] for i in range(N_MOD))
        S = x.shape[1]
        x1 = _ffn1(x, sh1, sc1, g1, row(ffn1_norm[0]), *f1)
        qa_t, ka, va_t, qb_t, kb, vb_t = _inproj(x1, sh2, sc2, row(mix_norm[0]), w1t, wuqt, wukvt,
                                                  gains, _rope_table(S))
        oa_t = _swa(qa_t, ka, va_t, sink_rows)
        ob_t = _mla(qb_t, kb, vb_t)
        return _ffn2(x1, oa_t.reshape(-1, A_Q, S), ob_t.reshape(-1, A_Q, S), woa, wob, g2, sh3, sc3, g3,
                     row(ffn2_norm[0]), *f2, row(final_norm[0]))

    y_prompt = trunk(x_prompt, mod[:bp])
    y_sample = trunk(x_sample, mod[bp:bp + bs])
    return (y_prompt, y_sample)
```

```python
import functools
import math

import jax
import jax.numpy as jnp
from jax import lax
from jax.experimental import pallas as pl
from jax.experimental.pallas import tpu as pltpu

F32 = jnp.float32
BF16 = jnp.bfloat16

D_MODEL = 1024
D_FF = 2816
N_HEADS_A = 8
N_KV_A = 2
HEAD_DIM_A = 64
GROUP_A = N_HEADS_A // N_KV_A
WINDOW = 128
N_HEADS_B = 8
Q_LORA = 256
KV_LORA = 128
NOPE_B = 64
ROPE_B = 32
QK_DIM_B = NOPE_B + ROPE_B
V_DIM_B = 64
ROPE_THETA = 10000.0
EPS = 1e-6
N_MOD = 9
A_Q = N_HEADS_A * HEAD_DIM_A
A_KV = N_KV_A * HEAD_DIM_A
IN_WIDTH = A_Q + 2 * A_KV + Q_LORA + KV_LORA + ROPE_B
NEG_BIG = -1e30
MLA_Q_SCALE = math.log2(math.e) / math.sqrt(QK_DIM_B)
MLA_MAX_SHIFT_RANGE = 100.0
BF16_ROUND_MARGIN = 1.02

LANE = 128
QK_PAD_B = LANE
VMEM_LIMIT_BYTES = 56 * 1024 * 1024

MOD_ROWS = 16
MOD_BN = 1152
FFN_TM = 512
FFN_FC = 256
FFN_NC = D_FF // FFN_FC
PROJ_TM = 512
SWA_TQ = 512
MLA_TQ = PROJ_TM
MLA_UNROLL = 4
MLA_TK = PROJ_TM
MLA_ONES_ROWS = 16

R_QA, R_KA, R_VA = 0, A_Q, A_Q + A_KV
R_CQ = A_Q + 2 * A_KV
R_CKV = R_CQ + Q_LORA
R_KR = R_CKV + KV_LORA
G_QA, G_KA = 0, HEAD_DIM_A
G_QL = 2 * HEAD_DIM_A
G_KVL = G_QL + Q_LORA
G_MQ = G_KVL + KV_LORA
G_KN = G_MQ + QK_PAD_B
G_KR = G_KN + NOPE_B
G_ROWS = G_KR + ROPE_B
T_CA, T_SA = 0, HEAD_DIM_A // 2
T_CB = HEAD_DIM_A
T_SB = T_CB + ROPE_B // 2
T_ROWS = T_SB + ROPE_B // 2

_NT = (((1,), (1,)), ((), ()))
_TN = (((0,), (0,)), ((), ()))


def _cparams(sem):
    return pltpu.CompilerParams(dimension_semantics=sem, vmem_limit_bytes=VMEM_LIMIT_BYTES)


def _const_spec(shape):
    n = len(shape)
    return pl.BlockSpec(shape, lambda *_: (0,) * n, pipeline_mode=pl.Buffered(1))


def _silu(x):
    return x * (1.0 / (1.0 + jnp.exp(-x)))


def _rms_rows(x):
    return lax.rsqrt(jnp.mean(x * x, axis=-1, keepdims=True) + EPS)


def _mod_kernel(c_ref, w_ref, b_ref, o_ref):
    a = _silu(c_ref[...]).astype(BF16)
    o_ref[...] = jnp.dot(a, w_ref[...].astype(BF16), preferred_element_type=F32) + b_ref[...]


def _modulation(c, ada_w, ada_b):
    n = ada_w.shape[1]
    return pl.pallas_call(
        _mod_kernel,
        grid=(n // MOD_BN,),
        in_specs=[
            pl.BlockSpec((MOD_ROWS, D_MODEL), lambda j: (0, 0)),
            pl.BlockSpec((D_MODEL, MOD_BN), lambda j: (0, j)),
            pl.BlockSpec((1, MOD_BN), lambda j: (0, j)),
        ],
        out_specs=pl.BlockSpec((MOD_ROWS, MOD_BN), lambda j: (0, j)),
        out_shape=jax.ShapeDtypeStruct((MOD_ROWS, n), F32),
        compiler_params=_cparams(("arbitrary",)),
        name="mod",
    )(c, ada_w, ada_b)


def _swiglu_into(acc_ref, h_ref, wg_ref, wu_ref, wd_ref):
    for c in range(FFN_NC):
        h = h_ref[...]
        g = jnp.dot(h, wg_ref[c], preferred_element_type=F32)
        u = jnp.dot(h, wu_ref[c], preferred_element_type=F32)
        a = (_silu(g) * u).astype(BF16)
        y = jnp.dot(a, wd_ref[c], preferred_element_type=F32)
        if c == 0:
            acc_ref[...] = y
        else:
            acc_ref[...] += y


def _ffn1_kernel(x_ref, sh_ref, sc_ref, gt_ref, nw_ref, wg_ref, wu_ref, wd_ref, o_ref, h_ref, acc_ref):
    x = x_ref[...]
    h = x * _rms_rows(x) * nw_ref[...]
    h_ref[...] = (h * (1.0 + sc_ref[...]) + sh_ref[...]).astype(BF16)
    _swiglu_into(acc_ref, h_ref, wg_ref, wu_ref, wd_ref)
    o_ref[...] = x_ref[...] + (0.5 * gt_ref[...]) * acc_ref[...]


def _ffn2_kernel(x_ref, oa_ref, ob_ref, woa_ref, wob_ref, g2_ref, sh_ref, sc_ref, gt_ref, nw_ref,
                 wg_ref, wu_ref, wd_ref, fw_ref, o_ref, h_ref, acc_ref, x2_ref):
    y = lax.dot_general(oa_ref[...], woa_ref[...], _TN, preferred_element_type=F32)
    ob = ob_ref[...].reshape(N_HEADS_B * V_DIM_B, ob_ref.shape[-1])
    y = y + lax.dot_general(ob, wob_ref[...], _TN, preferred_element_type=F32)
    x = x_ref[...] + g2_ref[...] * y
    x2_ref[...] = x
    h = x * _rms_rows(x) * nw_ref[...]
    h_ref[...] = (h * (1.0 + sc_ref[...]) + sh_ref[...]).astype(BF16)
    _swiglu_into(acc_ref, h_ref, wg_ref, wu_ref, wd_ref)
    z = x2_ref[...] + (0.5 * gt_ref[...]) * acc_ref[...]
    o_ref[...] = z * _rms_rows(z) * fw_ref[...]


def _row_spec():
    return pl.BlockSpec((None, 1, D_MODEL), lambda b, t: (b, 0, 0))


def _ffn_weight_specs():
    return [
        _const_spec((FFN_NC, D_MODEL, FFN_FC)),
        _const_spec((FFN_NC, D_MODEL, FFN_FC)),
        _const_spec((FFN_NC, FFN_FC, D_MODEL)),
    ]


def _ffn1(x, sh, sc, gt, nw, wg, wu, wd):
    B, S, _ = x.shape
    tm = FFN_TM
    tok = pl.BlockSpec((None, tm, D_MODEL), lambda b, t: (b, t, 0))
    return pl.pallas_call(
        _ffn1_kernel,
        grid=(B, S // tm),
        in_specs=[tok, _row_spec(), _row_spec(), _row_spec(), _const_spec((1, D_MODEL))] + _ffn_weight_specs(),
        out_specs=tok,
        out_shape=jax.ShapeDtypeStruct(x.shape, F32),
        scratch_shapes=[pltpu.VMEM((tm, D_MODEL), BF16), pltpu.VMEM((tm, D_MODEL), F32)],
        compiler_params=_cparams(("arbitrary", "arbitrary")),
        name="ffn1",
    )(x, sh, sc, gt, nw, wg, wu, wd)


def _ffn2(x, oa_t, ob_t, woa, wob, g2, sh, sc, gt, nw, wg, wu, wd, fw):
    B, S, _ = x.shape
    tm = FFN_TM
    tok = pl.BlockSpec((None, tm, D_MODEL), lambda b, t: (b, t, 0))
    att_a = pl.BlockSpec((None, A_Q, tm), lambda b, t: (b, 0, t))
    att_b = pl.BlockSpec((None, N_HEADS_B, None, V_DIM_B, tm), lambda b, t: (b, 0, t, 0, 0))
    return pl.pallas_call(
        _ffn2_kernel,
        grid=(B, S // tm),
        in_specs=[tok, att_a, att_b, _const_spec((A_Q, D_MODEL)), _const_spec((A_Q, D_MODEL)),
                  _row_spec(), _row_spec(), _row_spec(), _row_spec(), _const_spec((1, D_MODEL))]
                 + _ffn_weight_specs() + [_const_spec((1, D_MODEL))],
        out_specs=tok,
        out_shape=jax.ShapeDtypeStruct(x.shape, F32),
        scratch_shapes=[pltpu.VMEM((tm, D_MODEL), BF16), pltpu.VMEM((tm, D_MODEL), F32),
                        pltpu.VMEM((tm, D_MODEL), F32)],
        compiler_params=_cparams(("arbitrary", "arbitrary")),
        name="ffn2",
    )(x, oa_t, ob_t, woa, wob, g2, sh, sc, gt, nw, wg, wu, wd, fw)


def _rms_cols(x, n):
    return lax.rsqrt(jnp.sum(x * x, axis=0, keepdims=True) * (1.0 / n) + EPS)


def _rope_cols(y, cos, sin):
    half = y.shape[0] // 2
    y1, y2 = y[:half], y[half:]
    return jnp.concatenate([y1 * cos - y2 * sin, y2 * cos + y1 * sin], axis=0)


def _inproj_kernel(x_ref, sh_ref, sc_ref, nw_ref, w1_ref, wuq_ref, wukv_ref, g_ref, t_ref,
                   qa_ref, ka_ref, va_ref, qb_ref, kb_ref, vb_ref):
    tm = x_ref.shape[0]
    x = x_ref[...]
    h = x * _rms_rows(x) * nw_ref[...]
    h = (h * (1.0 + sc_ref[...]) + sh_ref[...]).astype(BF16)
    zt = lax.dot_general(w1_ref[...], h, _NT, preferred_element_type=F32)

    cos_a, sin_a = t_ref[T_CA:T_CA + 32, :], t_ref[T_SA:T_SA + 32, :]
    cos_b, sin_b = t_ref[T_CB:T_CB + 16, :], t_ref[T_SB:T_SB + 16, :]

    g_qa = g_ref[G_QA:G_QA + HEAD_DIM_A, :]
    for hd in range(N_HEADS_A):
        blk = zt[R_QA + hd * HEAD_DIM_A:R_QA + (hd + 1) * HEAD_DIM_A]
        y = blk * g_qa * _rms_cols(blk, HEAD_DIM_A)
        qa_ref[hd] = (_rope_cols(y, cos_a, sin_a) * (1.0 / math.sqrt(HEAD_DIM_A))).astype(BF16)
    g_ka = g_ref[G_KA:G_KA + HEAD_DIM_A, :]
    ks = []
    for hd in range(N_KV_A):
        blk = zt[R_KA + hd * HEAD_DIM_A:R_KA + (hd + 1) * HEAD_DIM_A]
        ks.append(_rope_cols(blk * g_ka * _rms_cols(blk, HEAD_DIM_A), cos_a, sin_a))
    ka_ref[...] = jnp.concatenate(ks, axis=0).T.astype(BF16)
    vt = zt[R_VA:R_VA + A_KV]
    for c in range(tm // LANE):
        va_ref[c] = vt[:, c * LANE:(c + 1) * LANE].astype(BF16)

    cq = zt[R_CQ:R_CQ + Q_LORA]
    cqn = (cq * g_ref[G_QL:G_QL + Q_LORA, :] * _rms_cols(cq, Q_LORA)).astype(BF16)
    qb_all = jnp.dot(wuq_ref[...], cqn, preferred_element_type=F32)
    g_mq = g_ref[G_MQ:G_MQ + QK_PAD_B, :]
    pad = jnp.zeros((QK_PAD_B - QK_DIM_B, tm), F32)
    for hd in range(N_HEADS_B):
        blk = qb_all[hd * QK_PAD_B:(hd + 1) * QK_PAD_B]
        y = blk * g_mq
        o = jnp.concatenate([y[:NOPE_B], _rope_cols(y[NOPE_B:QK_DIM_B], cos_b, sin_b), pad], axis=0)
        qb_ref[hd] = (o * (_rms_cols(blk, QK_DIM_B) * MLA_Q_SCALE)).astype(BF16)

    ckv = zt[R_CKV:R_CKV + KV_LORA]
    ckvn = (ckv * g_ref[G_KVL:G_KVL + KV_LORA, :] * _rms_cols(ckv, KV_LORA)).astype(BF16)
    kvt = jnp.dot(wukv_ref[...], ckvn, preferred_element_type=F32)
    kr = zt[R_KR:R_KR + ROPE_B]
    ss_kr = jnp.sum(kr * kr, axis=0, keepdims=True)
    kr_rot = _rope_cols(kr * g_ref[G_KR:G_KR + ROPE_B, :], cos_b, sin_b)
    g_kn = g_ref[G_KN:G_KN + NOPE_B, :]
    for hd in range(N_HEADS_B):
        kn = kvt[hd * 2 * NOPE_B:hd * 2 * NOPE_B + NOPE_B]
        r = lax.rsqrt((jnp.sum(kn * kn, axis=0, keepdims=True) + ss_kr) * (1.0 / QK_DIM_B) + EPS)
        kt = jnp.concatenate([kn * g_kn, kr_rot, pad], axis=0) * r
        kb_ref[hd] = kt.T.astype(BF16)
        vb_ref[hd] = kvt[hd * 2 * NOPE_B + NOPE_B:(hd + 1) * 2 * NOPE_B].astype(BF16)


def _inproj(x, sh, sc, nw, w1t, wuqt, wukvt, gains, table):
    B, S, _ = x.shape
    tm = PROJ_TM
    nt = S // tm
    out_shape = [
        jax.ShapeDtypeStruct((B, N_HEADS_A, HEAD_DIM_A, S), BF16),
        jax.ShapeDtypeStruct((B, S, A_KV), BF16),
        jax.ShapeDtypeStruct((B, S // LANE, A_KV, LANE), BF16),
        jax.ShapeDtypeStruct((B, N_HEADS_B, nt, QK_PAD_B, tm), BF16),
        jax.ShapeDtypeStruct((B, N_HEADS_B, S, QK_PAD_B), BF16),
        jax.ShapeDtypeStruct((B, N_HEADS_B, nt, V_DIM_B, tm), BF16),
    ]
    out_specs = [
        pl.BlockSpec((None, N_HEADS_A, HEAD_DIM_A, tm), lambda b, t: (b, 0, 0, t)),
        pl.BlockSpec((None, tm, A_KV), lambda b, t: (b, t, 0)),
        pl.BlockSpec((None, tm // LANE, A_KV, LANE), lambda b, t: (b, t, 0, 0)),
        pl.BlockSpec((None, N_HEADS_B, None, QK_PAD_B, tm), lambda b, t: (b, 0, t, 0, 0)),
        pl.BlockSpec((None, N_HEADS_B, tm, QK_PAD_B), lambda b, t: (b, 0, t, 0)),
        pl.BlockSpec((None, N_HEADS_B, None, V_DIM_B, tm), lambda b, t: (b, 0, t, 0, 0)),
    ]
    return pl.pallas_call(
        _inproj_kernel,
        grid=(B, nt),
        in_specs=[
            pl.BlockSpec((None, tm, D_MODEL), lambda b, t: (b, t, 0)),
            _row_spec(), _row_spec(), _const_spec((1, D_MODEL)),
            _const_spec((IN_WIDTH, D_MODEL)),
            _const_spec((N_HEADS_B * QK_PAD_B, Q_LORA)),
            _const_spec((N_HEADS_B * 2 * NOPE_B, KV_LORA)),
            _const_spec((G_ROWS, tm)),
            pl.BlockSpec((T_ROWS, tm), lambda b, t: (0, t)),
        ],
        out_specs=out_specs,
        out_shape=out_shape,
        compiler_params=_cparams(("arbitrary", "arbitrary")),
        name="inproj",
    )(x, sh, sc, nw, w1t, wuqt, wukvt, gains, table)


def _swa_kernel(q_ref, k_ref, v_ref, sink_ref, o_ref, *, n_blocks):
    tq = q_ref.shape[-1]
    nq = tq // LANE
    wide = GROUP_A * LANE
    row = lax.broadcasted_iota(jnp.int32, (LANE, wide), 0)
    col = lax.broadcasted_iota(jnp.int32, (LANE, wide), 1) & (LANE - 1)
    diff = row - col
    zeros = jnp.zeros((HEAD_DIM_A, LANE), BF16)
    for i in range(nq):
        nb = pl.program_id(1) * nq + i
        kb_idx = [jnp.maximum(nb - 1, 0), nb, jnp.minimum(nb + 1, n_blocks - 1)]
        lo = jnp.where(nb > 0, 0, 2 * LANE)
        hi = jnp.where(nb < n_blocks - 1, 0, -2 * LANE)
        masks = [diff >= lo, None, diff <= hi]
        for j in range(N_KV_A):
            cols = []
            for g in range(GROUP_A):
                qh = q_ref[j * GROUP_A + g, :, i * LANE:(i + 1) * LANE]
                cols.append(jnp.concatenate([qh, zeros] if j == 0 else [zeros, qh], axis=0))
            qp = jnp.concatenate(cols, axis=1)
            sink = sink_ref[j]
            ss = []
            m = sink
            for c in range(3):
                k = k_ref[pl.ds(pl.multiple_of(kb_idx[c] * LANE, LANE), LANE), :]
                s = jnp.dot(k, qp, preferred_element_type=F32)
                if masks[c] is not None:
                    s = jnp.where(masks[c], s, NEG_BIG)
                ss.append(s)
                m = jnp.maximum(m, jnp.max(s, axis=0, keepdims=True))
            l = jnp.exp(sink - m)
            acc = jnp.zeros((HEAD_DIM_A, wide), F32)
            for c in range(3):
                p = jnp.exp(ss[c] - m)
                l = l + jnp.sum(p, axis=0, keepdims=True)
                vt = v_ref[kb_idx[c], j * HEAD_DIM_A:(j + 1) * HEAD_DIM_A, :]
                acc = acc + jnp.dot(vt, p.astype(BF16), preferred_element_type=F32)
            o = acc * (1.0 / l)
            for g in range(GROUP_A):
                o_ref[j * GROUP_A + g, :, i * LANE:(i + 1) * LANE] = o[:, g * LANE:(g + 1) * LANE].astype(BF16)


def _swa(qa_t, ka, va_t, sink_rows):
    B, _, _, S = qa_t.shape
    tq = SWA_TQ
    nbk = S // LANE
    qspec = pl.BlockSpec((None, N_HEADS_A, HEAD_DIM_A, tq), lambda b, t: (b, 0, 0, t))
    return pl.pallas_call(
        functools.partial(_swa_kernel, n_blocks=nbk),
        grid=(B, S // tq),
        in_specs=[
            qspec,
            pl.BlockSpec((None, S, A_KV), lambda b, t: (b, 0, 0)),
            pl.BlockSpec((None, nbk, A_KV, LANE), lambda b, t: (b, 0, 0, 0)),
            _const_spec((N_KV_A, 1, GROUP_A * LANE)),
        ],
        out_specs=qspec,
        out_shape=jax.ShapeDtypeStruct(qa_t.shape, BF16),
        compiler_params=_cparams(("arbitrary", "arbitrary")),
        name="swa",
    )(qa_t, ka, va_t, sink_rows)


def _mla_online_kernel(q_ref, k_ref, v_ref, o_ref, m_ref, acc_ref):
    nq = q_ref.shape[0]
    n = v_ref.shape[0]
    tk = v_ref.shape[-1]
    ones = jnp.ones((MLA_ONES_ROWS, tk), BF16)

    def qblock(qb, carry):
        m_ref[...] = jnp.full_like(m_ref, NEG_BIG)
        acc_ref[...] = jnp.zeros_like(acc_ref)

        def chunk(c, carry2):
            k = k_ref[pl.ds(pl.multiple_of(c * tk, tk), tk), :]
            s = jnp.dot(k, q_ref[qb], preferred_element_type=F32)
            m_old = m_ref[...]
            m_new = jnp.maximum(m_old, jnp.max(s, axis=0, keepdims=True))
            p = jnp.exp2(s - m_new).astype(BF16)
            v1 = jnp.concatenate([v_ref[c], ones], axis=0)
            acc_ref[...] = jnp.exp2(m_old - m_new) * acc_ref[...] + jnp.dot(v1, p, preferred_element_type=F32)
            m_ref[...] = m_new
            return carry2

        lax.fori_loop(0, n, chunk, 0)
        acc = acc_ref[...]
        o_ref[qb] = (acc[:V_DIM_B] * (1.0 / acc[V_DIM_B:V_DIM_B + 1])).astype(BF16)
        return carry

    lax.fori_loop(0, nq, qblock, 0)


def _mla_bounded_kernel(shift_ref, q_ref, k_ref, v_ref, o_ref, p_ref, acc_ref):
    nq = q_ref.shape[0]
    n = v_ref.shape[0]
    tk = v_ref.shape[-1]
    total = nq * n
    log2n = n.bit_length() - 1
    shift = shift_ref[0, 0]
    ones = jnp.ones((MLA_ONES_ROWS, tk), BF16)

    def probs(t, slot):
        c = t & (n - 1)
        k = k_ref[pl.ds(pl.multiple_of(c * tk, tk), tk), :]
        s = jnp.dot(k, q_ref[t >> log2n], preferred_element_type=F32)
        p_ref[slot] = jnp.exp2(s - shift).astype(BF16)

    def pv(t, slot):
        v1 = jnp.concatenate([v_ref[t & (n - 1)], ones], axis=0)
        acc_ref[...] += jnp.dot(v1, p_ref[slot], preferred_element_type=F32)

    acc_ref[...] = jnp.zeros_like(acc_ref)
    probs(0, 0)

    def body(j, carry):
        t0 = j * MLA_UNROLL
        for u in range(MLA_UNROLL):
            probs(jnp.minimum(t0 + u + 1, total - 1), (u + 1) % 2)
            pv(t0 + u, u % 2)

        @pl.when(((t0 + MLA_UNROLL) & (n - 1)) == 0)
        def _():
            acc = acc_ref[...]
            o_ref[(t0 + MLA_UNROLL - 1) >> log2n] = (acc[:V_DIM_B] * (1.0 / acc[V_DIM_B:V_DIM_B + 1])).astype(BF16)
            acc_ref[...] = jnp.zeros_like(acc_ref)

        return carry

    lax.fori_loop(0, total // MLA_UNROLL, body, 0)


def _mla_call(kernel_fn, scratch, qb_t, kb, vb_t, *scalars):
    B, H, nq, _, tq = qb_t.shape
    nck, tk = vb_t.shape[2], vb_t.shape[4]
    S = nck * tk
    assert nck & (nck - 1) == 0 and nck % MLA_UNROLL == 0 and MLA_UNROLL % 2 == 0
    return pl.pallas_call(
        kernel_fn,
        grid=(B, H),
        in_specs=[pl.BlockSpec(memory_space=pltpu.SMEM)] * len(scalars) + [
            pl.BlockSpec((None, None, nq, QK_PAD_B, tq), lambda b, h: (b, h, 0, 0, 0)),
            pl.BlockSpec((None, None, S, QK_PAD_B), lambda b, h: (b, h, 0, 0)),
            pl.BlockSpec((None, None, nck, V_DIM_B, tk), lambda b, h: (b, h, 0, 0, 0)),
        ],
        out_specs=pl.BlockSpec((None, None, nq, V_DIM_B, tq), lambda b, h: (b, h, 0, 0, 0)),
        out_shape=jax.ShapeDtypeStruct((B, H, nq, V_DIM_B, tq), BF16),
        scratch_shapes=scratch(tk, tq),
        compiler_params=_cparams(("arbitrary", "arbitrary")),
        name=kernel_fn.__name__.strip("_").replace("_kernel", ""),
    )(*scalars, qb_t, kb, vb_t)


def _mla_online_scratch(tk, tq):
    return [pltpu.VMEM((1, tq), F32), pltpu.VMEM((V_DIM_B + MLA_ONES_ROWS, tq), F32)]


def _mla_bounded_scratch(tk, tq):
    return [pltpu.VMEM((2, tk, tq), BF16), pltpu.VMEM((V_DIM_B + MLA_ONES_ROWS, tq), F32)]


def _mla(qb_t, kb, vb_t, score_bound):
    shift = score_bound.reshape(1, 1).astype(F32)
    return lax.cond(
        2.0 * score_bound <= MLA_MAX_SHIFT_RANGE,
        lambda: _mla_call(_mla_bounded_kernel, _mla_bounded_scratch, qb_t, kb, vb_t, shift),
        lambda: _mla_call(_mla_online_kernel, _mla_online_scratch, qb_t, kb, vb_t),
    )


def _rope_table(S):
    pos = jnp.arange(S, dtype=F32)

    def cs(d):
        inv = ROPE_THETA ** (-jnp.arange(0, d, 2, dtype=F32) / d)
        ang = pos[:, None] * inv[None, :]
        return jnp.cos(ang).T, jnp.sin(ang).T

    ca, sa = cs(HEAD_DIM_A)
    cb, sb = cs(ROPE_B)
    return jnp.concatenate([ca, sa, cb, sb], axis=0)


def _gain_tile(swa_q_norm, swa_k_norm, mla_q_lora_norm, mla_kv_lora_norm, mla_q_norm, mla_k_norm):
    rows = jnp.concatenate([
        swa_q_norm, swa_k_norm, mla_q_lora_norm, mla_kv_lora_norm,
        mla_q_norm, jnp.zeros((QK_PAD_B - QK_DIM_B,), F32),
        mla_k_norm[:NOPE_B], mla_k_norm[NOPE_B:],
    ]).astype(F32)
    return jnp.broadcast_to(rows[:, None], (G_ROWS, PROJ_TM))


def _chunk_cols(w):
    d, f = w.shape
    return w.reshape(d, f // FFN_FC, FFN_FC).transpose(1, 0, 2).astype(BF16)


def _chunk_rows(w):
    f, d = w.shape
    return w.reshape(f // FFN_FC, FFN_FC, d).astype(BF16)


def kernel(x_prompt, x_sample, c_prompt, c_sample, ada_w, ada_b, ffn1_norm, ffn1_wg, ffn1_wu, ffn1_wd, mix_norm, w_in, swa_q_norm, swa_k_norm, swa_sink, mla_q_lora_norm, mla_w_uq, mla_kv_lora_norm, mla_w_ukv, mla_q_norm, mla_k_norm, w_out, ffn2_norm, ffn2_wg, ffn2_wu, ffn2_wd, final_norm):
    assert ada_w.shape[0] == 1, "single-layer trunk"
    bp, bs = x_prompt.shape[0], x_sample.shape[0]
    assert bp + bs <= MOD_ROWS

    f1 = (_chunk_cols(ffn1_wg[0]), _chunk_cols(ffn1_wu[0]), _chunk_rows(ffn1_wd[0]))
    f2 = (_chunk_cols(ffn2_wg[0]), _chunk_cols(ffn2_wu[0]), _chunk_rows(ffn2_wd[0]))
    w1t = w_in[0].T.astype(BF16)
    wuq = mla_w_uq[0].reshape(Q_LORA, N_HEADS_B, QK_DIM_B)
    wuq = jnp.pad(wuq, ((0, 0), (0, 0), (0, QK_PAD_B - QK_DIM_B))).reshape(Q_LORA, N_HEADS_B * QK_PAD_B)
    wuqt = wuq.T.astype(BF16)
    wukvt = mla_w_ukv[0].T.astype(BF16)
    woa = w_out[0, :A_Q].astype(BF16)
    wob = w_out[0, A_Q:].astype(BF16)
    gains = _gain_tile(swa_q_norm[0], swa_k_norm[0], mla_q_lora_norm[0], mla_kv_lora_norm[0],
                       mla_q_norm[0], mla_k_norm[0])
    sink_rows = jnp.repeat(swa_sink[0].astype(F32), LANE).reshape(N_KV_A, 1, GROUP_A * LANE)
    row = lambda v: v.reshape(1, D_MODEL)
    score_bound = (BF16_ROUND_MARGIN * QK_DIM_B * MLA_Q_SCALE
                   * jnp.max(jnp.abs(mla_q_norm[0])) * jnp.max(jnp.abs(mla_k_norm[0]))).astype(F32)

    c_all = jnp.concatenate([c_prompt, c_sample, jnp.zeros((MOD_ROWS - bp - bs, D_MODEL), F32)], axis=0)
    mod = _modulation(c_all, ada_w[0], ada_b[0].reshape(1, -1)).reshape(MOD_ROWS, N_MOD, 1, D_MODEL)

    def trunk(x, m):
        sh1, sc1, g1, sh2, sc2, g2, sh3, sc3, g3 = (m[:, i] for i in range(N_MOD))
        S = x.shape[1]
        x1 = _ffn1(x, sh1, sc1, g1, row(ffn1_norm[0]), *f1)
        qa_t, ka, va_t, qb_t, kb, vb_t = _inproj(x1, sh2, sc2, row(mix_norm[0]), w1t, wuqt, wukvt,
                                                  gains, _rope_table(S))
        oa_t = _swa(qa_t, ka, va_t, sink_rows)
        ob_t = _mla(qb_t, kb, vb_t, score_bound)
        return _ffn2(x1, oa_t.reshape(-1, A_Q, S), ob_t, woa, wob, g2, sh3, sc3, g3,
                     row(ffn2_norm[0]), *f2, row(final_norm[0]))

    y_prompt = trunk(x_prompt, mod[:bp])
    y_sample = trunk(x_sample, mod[bp:bp + bs])
    return (y_prompt, y_sample)
```

```python
import functools
import math

import jax
import jax.numpy as jnp
from jax import lax
from jax.experimental import pallas as pl
from jax.experimental.pallas import tpu as pltpu

F32 = jnp.float32
BF16 = jnp.bfloat16

D_MODEL = 1024
D_FF = 2816
N_HEADS_A = 8
N_KV_A = 2
HEAD_DIM_A = 64
GROUP_A = N_HEADS_A // N_KV_A
WINDOW = 128
N_HEADS_B = 8
Q_LORA = 256
KV_LORA = 128
NOPE_B = 64
ROPE_B = 32
QK_DIM_B = NOPE_B + ROPE_B
V_DIM_B = 64
ROPE_THETA = 10000.0
EPS = 1e-6
N_MOD = 9
A_Q = N_HEADS_A * HEAD_DIM_A
A_KV = N_KV_A * HEAD_DIM_A
IN_WIDTH = A_Q + 2 * A_KV + Q_LORA + KV_LORA + ROPE_B
NEG_BIG = -1e30
LOG2E = math.log2(math.e)
SWA_Q_SCALE = LOG2E / math.sqrt(HEAD_DIM_A)
MLA_Q_SCALE = LOG2E / math.sqrt(QK_DIM_B)
SOFTMAX_MAX_SHIFT_RANGE = 100.0
BF16_ROUND_MARGIN = 1.02

LANE = 128
QK_PAD_B = LANE
VMEM_LIMIT_BYTES = 56 * 1024 * 1024

MOD_ROWS = 16
MOD_BN = 1152
FFN_TM = 512
FFN_FC = 256
FFN_NC = D_FF // FFN_FC
PROJ_TM = 1024
PROJ_SUB = 512
SWA_TQ = 512
MLA_TQ = 512
MLA_MAX_UNROLL = 16
MLA_TK = 512
ONES_ROWS = 16
PV_LHS_ROWS = 128

R_QA, R_KA, R_VA = 0, A_Q, A_Q + A_KV
R_CQ = A_Q + 2 * A_KV
R_CKV = R_CQ + Q_LORA
R_KR = R_CKV + KV_LORA
G_QA, G_KA = 0, HEAD_DIM_A
G_QL = 2 * HEAD_DIM_A
G_KVL = G_QL + Q_LORA
G_MQ = G_KVL + KV_LORA
G_KN = G_MQ + QK_DIM_B
G_KR = G_KN + NOPE_B
G_ROWS = G_KR + ROPE_B
T_CA, T_SA = 0, HEAD_DIM_A // 2
T_CB = HEAD_DIM_A
T_SB = T_CB + ROPE_B // 2
T_ROWS = T_SB + ROPE_B // 2

_NT = (((1,), (1,)), ((), ()))
_TN = (((0,), (0,)), ((), ()))


def _cparams(sem):
    return pltpu.CompilerParams(dimension_semantics=sem, vmem_limit_bytes=VMEM_LIMIT_BYTES)


def _const_spec(shape):
    n = len(shape)
    return pl.BlockSpec(shape, lambda *_: (0,) * n, pipeline_mode=pl.Buffered(1))


def _silu(x):
    return x * (1.0 / (1.0 + jnp.exp(-x)))


def _rms_rows(x):
    return lax.rsqrt(jnp.mean(x * x, axis=-1, keepdims=True) + EPS)


def _mod_kernel(c_ref, w_ref, b_ref, o_ref):
    a = _silu(c_ref[...]).astype(BF16)
    o_ref[...] = jnp.dot(a, w_ref[...].astype(BF16), preferred_element_type=F32) + b_ref[...]


def _modulation(c, ada_w, ada_b):
    n = ada_w.shape[1]
    return pl.pallas_call(
        _mod_kernel,
        grid=(n // MOD_BN,),
        in_specs=[
            pl.BlockSpec((MOD_ROWS, D_MODEL), lambda j: (0, 0)),
            pl.BlockSpec((D_MODEL, MOD_BN), lambda j: (0, j)),
            pl.BlockSpec((1, MOD_BN), lambda j: (0, j)),
        ],
        out_specs=pl.BlockSpec((MOD_ROWS, MOD_BN), lambda j: (0, j)),
        out_shape=jax.ShapeDtypeStruct((MOD_ROWS, n), F32),
        compiler_params=_cparams(("arbitrary",)),
        name="mod",
    )(c, ada_w, ada_b)


def _swiglu_into(acc_ref, h_ref, wg_ref, wu_ref, wd_ref):
    for c in range(FFN_NC):
        cols = slice(c * FFN_FC, (c + 1) * FFN_FC)
        h = h_ref[...]
        g = jnp.dot(h, wg_ref[:, cols], preferred_element_type=F32)
        u = jnp.dot(h, wu_ref[:, cols], preferred_element_type=F32)
        a = (_silu(g) * u).astype(BF16)
        y = jnp.dot(a, wd_ref[cols, :], preferred_element_type=F32)
        if c == 0:
            acc_ref[...] = y
        else:
            acc_ref[...] += y


def _ffn1_kernel(x_ref, sh_ref, sc_ref, gt_ref, nw_ref, wg_ref, wu_ref, wd_ref, o_ref, h_ref, acc_ref):
    x = x_ref[...]
    h = x * _rms_rows(x) * nw_ref[...]
    h_ref[...] = (h * (1.0 + sc_ref[...]) + sh_ref[...]).astype(BF16)
    _swiglu_into(acc_ref, h_ref, wg_ref, wu_ref, wd_ref)
    o_ref[...] = x_ref[...] + (0.5 * gt_ref[...]) * acc_ref[...]


def _ffn2_kernel(x_ref, oa_ref, ob_ref, woa_ref, wob_ref, g2_ref, sh_ref, sc_ref, gt_ref, nw_ref,
                 wg_ref, wu_ref, wd_ref, fw_ref, o_ref, h_ref, acc_ref, x2_ref):
    y = lax.dot_general(oa_ref[...], woa_ref[...], _TN, preferred_element_type=F32)
    ob = ob_ref[...].reshape(N_HEADS_B * V_DIM_B, ob_ref.shape[-1])
    y = y + lax.dot_general(ob, wob_ref[...], _TN, preferred_element_type=F32)
    x = x_ref[...] + g2_ref[...] * y
    x2_ref[...] = x
    h = x * _rms_rows(x) * nw_ref[...]
    h_ref[...] = (h * (1.0 + sc_ref[...]) + sh_ref[...]).astype(BF16)
    _swiglu_into(acc_ref, h_ref, wg_ref, wu_ref, wd_ref)
    z = x2_ref[...] + (0.5 * gt_ref[...]) * acc_ref[...]
    o_ref[...] = z * _rms_rows(z) * fw_ref[...]


def _row_spec():
    return pl.BlockSpec((None, 1, D_MODEL), lambda b, t: (b, 0, 0))


def _ffn_weight_specs():
    return [_const_spec((D_MODEL, D_FF)), _const_spec((D_MODEL, D_FF)), _const_spec((D_FF, D_MODEL))]


def _ffn1(x, sh, sc, gt, nw, wg, wu, wd):
    B, S, _ = x.shape
    tm = FFN_TM
    tok = pl.BlockSpec((None, tm, D_MODEL), lambda b, t: (b, t, 0))
    return pl.pallas_call(
        _ffn1_kernel,
        grid=(B, S // tm),
        in_specs=[tok, _row_spec(), _row_spec(), _row_spec(), _const_spec((1, D_MODEL))] + _ffn_weight_specs(),
        out_specs=tok,
        out_shape=jax.ShapeDtypeStruct(x.shape, F32),
        scratch_shapes=[pltpu.VMEM((tm, D_MODEL), BF16), pltpu.VMEM((tm, D_MODEL), F32)],
        compiler_params=_cparams(("arbitrary", "arbitrary")),
        name="ffn1",
    )(x, sh, sc, gt, nw, wg, wu, wd)


def _ffn2(x, oa_t, ob_t, woa, wob, g2, sh, sc, gt, nw, wg, wu, wd, fw):
    B, S, _ = x.shape
    tm = FFN_TM
    assert tm == MLA_TQ
    tok = pl.BlockSpec((None, tm, D_MODEL), lambda b, t: (b, t, 0))
    att_a = pl.BlockSpec((None, A_Q, tm), lambda b, t: (b, 0, t))
    att_b = pl.BlockSpec((None, N_HEADS_B, None, V_DIM_B, tm), lambda b, t: (b, 0, t, 0, 0))
    return pl.pallas_call(
        _ffn2_kernel,
        grid=(B, S // tm),
        in_specs=[tok, att_a, att_b, _const_spec((A_Q, D_MODEL)), _const_spec((A_Q, D_MODEL)),
                  _row_spec(), _row_spec(), _row_spec(), _row_spec(), _const_spec((1, D_MODEL))]
                 + _ffn_weight_specs() + [_const_spec((1, D_MODEL))],
        out_specs=tok,
        out_shape=jax.ShapeDtypeStruct(x.shape, F32),
        scratch_shapes=[pltpu.VMEM((tm, D_MODEL), BF16), pltpu.VMEM((tm, D_MODEL), F32),
                        pltpu.VMEM((tm, D_MODEL), F32)],
        compiler_params=_cparams(("arbitrary", "arbitrary")),
        name="ffn2",
    )(x, oa_t, ob_t, woa, wob, g2, sh, sc, gt, nw, wg, wu, wd, fw)


def _rms_cols(x, n):
    return lax.rsqrt(jnp.sum(x * x, axis=0, keepdims=True) * (1.0 / n) + EPS)


def _rope_cols(y, cos, sin):
    half = y.shape[0] // 2
    y1, y2 = y[:half], y[half:]
    return jnp.concatenate([y1 * cos - y2 * sin, y2 * cos + y1 * sin], axis=0)


def _inproj_sub(off, x_ref, sh_ref, sc_ref, nw_ref, w1_ref, wuq_ref, wukv_ref, g_ref, t_ref,
                qa_ref, ka_ref, va_ref, qb_ref, kb_ref, vb_ref):
    n = PROJ_SUB
    tok = slice(off, off + n)
    x = x_ref[tok, :]
    h = x * _rms_rows(x) * nw_ref[...]
    h = (h * (1.0 + sc_ref[...]) + sh_ref[...]).astype(BF16)
    zt = lax.dot_general(w1_ref[...], h, _NT, preferred_element_type=F32)

    cos_a, sin_a = t_ref[T_CA:T_CA + 32, tok], t_ref[T_SA:T_SA + 32, tok]
    cos_b, sin_b = t_ref[T_CB:T_CB + 16, tok], t_ref[T_SB:T_SB + 16, tok]

    g_qa = g_ref[G_QA:G_QA + HEAD_DIM_A, :]
    for hd in range(N_HEADS_A):
        blk = zt[R_QA + hd * HEAD_DIM_A:R_QA + (hd + 1) * HEAD_DIM_A]
        r = _rms_cols(blk, HEAD_DIM_A) * SWA_Q_SCALE
        qa_ref[hd, :, tok] = (_rope_cols(blk * g_qa, cos_a, sin_a) * r).astype(BF16)
    g_ka = g_ref[G_KA:G_KA + HEAD_DIM_A, :]
    ks = []
    for hd in range(N_KV_A):
        blk = zt[R_KA + hd * HEAD_DIM_A:R_KA + (hd + 1) * HEAD_DIM_A]
        ks.append(_rope_cols(blk * g_ka, cos_a, sin_a) * _rms_cols(blk, HEAD_DIM_A))
    ka_ref[tok, :] = jnp.concatenate(ks, axis=0).T.astype(BF16)
    vt = zt[R_VA:R_VA + A_KV].astype(BF16)
    for c in range(n // LANE):
        va_ref[off // LANE + c] = vt[:, c * LANE:(c + 1) * LANE]

    cq = zt[R_CQ:R_CQ + Q_LORA]
    cqn = (cq * g_ref[G_QL:G_QL + Q_LORA, :] * _rms_cols(cq, Q_LORA)).astype(BF16)
    qb_all = jnp.dot(wuq_ref[...], cqn, preferred_element_type=F32)
    g_mq = g_ref[G_MQ:G_MQ + QK_DIM_B, :]
    pad = jnp.zeros((QK_PAD_B - QK_DIM_B, n), BF16)
    for hd in range(N_HEADS_B):
        blk = qb_all[hd * QK_PAD_B:hd * QK_PAD_B + QK_DIM_B]
        y = blk * g_mq
        o = jnp.concatenate([y[:NOPE_B], _rope_cols(y[NOPE_B:], cos_b, sin_b)], axis=0)
        o = (o * (_rms_cols(blk, QK_DIM_B) * MLA_Q_SCALE)).astype(BF16)
        step = min(n, MLA_TQ)
        for c in range(n // step):
            lo = off + c * step
            qb_ref[hd, lo // MLA_TQ, :, lo % MLA_TQ:lo % MLA_TQ + step] = (
                jnp.concatenate([o[:, c * step:(c + 1) * step], pad[:, :step]], axis=0))

    ckv = zt[R_CKV:R_CKV + KV_LORA]
    ckvn = (ckv * g_ref[G_KVL:G_KVL + KV_LORA, :] * _rms_cols(ckv, KV_LORA)).astype(BF16)
    kvt = jnp.dot(wukv_ref[...], ckvn, preferred_element_type=F32)
    kr = zt[R_KR:R_KR + ROPE_B]
    ss_kr = jnp.sum(kr * kr, axis=0, keepdims=True)
    kr_rot = _rope_cols(kr * g_ref[G_KR:G_KR + ROPE_B, :], cos_b, sin_b)
    g_kn = g_ref[G_KN:G_KN + NOPE_B, :]
    padf = jnp.zeros((QK_PAD_B - QK_DIM_B, n), F32)
    for hd in range(N_HEADS_B):
        kn = kvt[hd * 2 * NOPE_B:hd * 2 * NOPE_B + NOPE_B]
        r = lax.rsqrt((jnp.sum(kn * kn, axis=0, keepdims=True) + ss_kr) * (1.0 / QK_DIM_B) + EPS)
        kt = jnp.concatenate([kn * g_kn * r, kr_rot * r, padf], axis=0)
        kb_ref[hd, tok, :] = kt.T.astype(BF16)
        v = kvt[hd * 2 * NOPE_B + NOPE_B:(hd + 1) * 2 * NOPE_B].astype(BF16)
        step = min(n, MLA_TK)
        for c in range(n // step):
            lo = off + c * step
            vb_ref[hd, lo // MLA_TK, :, lo % MLA_TK:lo % MLA_TK + step] = v[:, c * step:(c + 1) * step]


def _inproj_kernel(*refs):
    for off in range(0, PROJ_TM, PROJ_SUB):
        _inproj_sub(off, *refs)


def _inproj(x, sh, sc, nw, w1t, wuqt, wukvt, gains, table):
    B, S, _ = x.shape
    tm = PROJ_TM
    nt = S // tm
    out_shape = [
        jax.ShapeDtypeStruct((B, N_HEADS_A, HEAD_DIM_A, S), BF16),
        jax.ShapeDtypeStruct((B, S, A_KV), BF16),
        jax.ShapeDtypeStruct((B, S // LANE, A_KV, LANE), BF16),
        jax.ShapeDtypeStruct((B, N_HEADS_B, S // MLA_TQ, QK_PAD_B, MLA_TQ), BF16),
        jax.ShapeDtypeStruct((B, N_HEADS_B, S, QK_PAD_B), BF16),
        jax.ShapeDtypeStruct((B, N_HEADS_B, S // MLA_TK, V_DIM_B, MLA_TK), BF16),
    ]
    out_specs = [
        pl.BlockSpec((None, N_HEADS_A, HEAD_DIM_A, tm), lambda b, t: (b, 0, 0, t)),
        pl.BlockSpec((None, tm, A_KV), lambda b, t: (b, t, 0)),
        pl.BlockSpec((None, tm // LANE, A_KV, LANE), lambda b, t: (b, t, 0, 0)),
        pl.BlockSpec((None, N_HEADS_B, tm // MLA_TQ, QK_PAD_B, MLA_TQ), lambda b, t: (b, 0, t, 0, 0)),
        pl.BlockSpec((None, N_HEADS_B, tm, QK_PAD_B), lambda b, t: (b, 0, t, 0)),
        pl.BlockSpec((None, N_HEADS_B, tm // MLA_TK, V_DIM_B, MLA_TK), lambda b, t: (b, 0, t, 0, 0)),
    ]
    return pl.pallas_call(
        _inproj_kernel,
        grid=(B, nt),
        in_specs=[
            pl.BlockSpec((None, tm, D_MODEL), lambda b, t: (b, t, 0)),
            _row_spec(), _row_spec(), _const_spec((1, D_MODEL)),
            _const_spec((IN_WIDTH, D_MODEL)),
            _const_spec((N_HEADS_B * QK_PAD_B, Q_LORA)),
            _const_spec((N_HEADS_B * 2 * NOPE_B, KV_LORA)),
            _const_spec((G_ROWS, PROJ_SUB)),
            pl.BlockSpec((T_ROWS, tm), lambda b, t: (0, t)),
        ],
        out_specs=out_specs,
        out_shape=out_shape,
        compiler_params=_cparams(("arbitrary", "arbitrary")),
        name="inproj",
    )(x, sh, sc, nw, w1t, wuqt, wukvt, gains, table)


def _swa_kernel(*refs, n_blocks, bounded):
    if bounded:
        shift_ref, q_ref, k_ref, v_ref, sink_ref, o_ref = refs
    else:
        q_ref, k_ref, v_ref, sink_ref, o_ref = refs
    tq = q_ref.shape[-1]
    nq = tq // LANE
    wide = GROUP_A * LANE
    row = lax.broadcasted_iota(jnp.int32, (LANE, wide), 0)
    col = lax.broadcasted_iota(jnp.int32, (LANE, wide), 1) & (LANE - 1)
    diff = row - col
    zeros = jnp.zeros((HEAD_DIM_A, LANE), BF16)
    ones = jnp.ones((ONES_ROWS, LANE), BF16)
    for i in range(nq):
        nb = pl.program_id(1) * nq + i
        kb_idx = [jnp.maximum(nb - 1, 0), nb, jnp.minimum(nb + 1, n_blocks - 1)]
        lo = jnp.where(nb > 0, 0, 2 * LANE)
        hi = jnp.where(nb < n_blocks - 1, 0, -2 * LANE)
        masks = [diff >= lo, None, diff <= hi]
        for j in range(N_KV_A):
            cols = []
            for g in range(GROUP_A):
                qh = q_ref[j * GROUP_A + g, :, i * LANE:(i + 1) * LANE]
                cols.append(jnp.concatenate([qh, zeros] if j == 0 else [zeros, qh], axis=0))
            qp = jnp.concatenate(cols, axis=1)
            sink = sink_ref[j]
            ss = []
            for c in range(3):
                k = k_ref[pl.ds(pl.multiple_of(kb_idx[c] * LANE, LANE), LANE), :]
                ss.append(jnp.dot(k, qp, preferred_element_type=F32))
            if bounded:
                m = shift_ref[0, 0]
            else:
                m = sink
                for c in range(3):
                    sc = ss[c] if masks[c] is None else jnp.where(masks[c], ss[c], NEG_BIG)
                    m = jnp.maximum(m, jnp.max(sc, axis=0, keepdims=True))
            acc = None
            for c in range(3):
                p = jnp.exp2(ss[c] - m)
                if masks[c] is not None:
                    p = jnp.where(masks[c], p, 0.0)
                vt = v_ref[kb_idx[c], j * HEAD_DIM_A:(j + 1) * HEAD_DIM_A, :]
                pv = jnp.dot(jnp.concatenate([vt, ones], axis=0), p.astype(BF16), preferred_element_type=F32)
                acc = pv if acc is None else acc + pv
            l = acc[HEAD_DIM_A:HEAD_DIM_A + 1] + jnp.exp2(sink - m)
            o = acc[:HEAD_DIM_A] * (1.0 / l)
            for g in range(GROUP_A):
                o_ref[j * GROUP_A + g, :, i * LANE:(i + 1) * LANE] = o[:, g * LANE:(g + 1) * LANE].astype(BF16)


def _swa_call(bounded, qa_t, ka, va_t, sink_rows, *scalars):
    B, _, _, S = qa_t.shape
    tq = SWA_TQ
    nbk = S // LANE
    qspec = pl.BlockSpec((None, N_HEADS_A, HEAD_DIM_A, tq), lambda b, t: (b, 0, 0, t))
    return pl.pallas_call(
        functools.partial(_swa_kernel, n_blocks=nbk, bounded=bounded),
        grid=(B, S // tq),
        in_specs=[pl.BlockSpec(memory_space=pltpu.SMEM)] * len(scalars) + [
            qspec,
            pl.BlockSpec((None, S, A_KV), lambda b, t: (b, 0, 0)),
            pl.BlockSpec((None, nbk, A_KV, LANE), lambda b, t: (b, 0, 0, 0)),
            _const_spec((N_KV_A, 1, GROUP_A * LANE)),
        ],
        out_specs=qspec,
        out_shape=jax.ShapeDtypeStruct(qa_t.shape, BF16),
        compiler_params=_cparams(("arbitrary", "arbitrary")),
        name="swa_bounded" if bounded else "swa_online",
    )(*scalars, qa_t, ka, va_t, sink_rows)


def _swa(qa_t, ka, va_t, sink_rows, score_bound):
    shift = jnp.maximum(score_bound, jnp.max(sink_rows))
    return lax.cond(
        shift + score_bound <= SOFTMAX_MAX_SHIFT_RANGE,
        lambda: _swa_call(True, qa_t, ka, va_t, sink_rows, shift.reshape(1, 1).astype(F32)),
        lambda: _swa_call(False, qa_t, ka, va_t, sink_rows),
    )


def _mla_online_kernel(q_ref, k_ref, v_ref, o_ref, m_ref, acc_ref):
    nq = q_ref.shape[0]
    n = v_ref.shape[0]
    tk = v_ref.shape[-1]
    ones = jnp.ones((ONES_ROWS, tk), BF16)

    def qblock(qb, carry):
        m_ref[...] = jnp.full_like(m_ref, NEG_BIG)
        acc_ref[...] = jnp.zeros_like(acc_ref)

        def chunk(c, carry2):
            k = k_ref[pl.ds(pl.multiple_of(c * tk, tk), tk), :]
            s = jnp.dot(k, q_ref[qb], preferred_element_type=F32)
            m_old = m_ref[...]
            m_new = jnp.maximum(m_old, jnp.max(s, axis=0, keepdims=True))
            p = jnp.exp2(s - m_new).astype(BF16)
            v1 = jnp.concatenate([v_ref[c], ones], axis=0)
            acc_ref[...] = jnp.exp2(m_old - m_new) * acc_ref[...] + jnp.dot(v1, p, preferred_element_type=F32)
            m_ref[...] = m_new
            return carry2

        lax.fori_loop(0, n, chunk, 0)
        acc = acc_ref[...]
        o_ref[qb] = (acc[:V_DIM_B] * (1.0 / acc[V_DIM_B:V_DIM_B + 1])).astype(BF16)
        return carry

    lax.fori_loop(0, nq, qblock, 0)


def _mla_bounded_kernel(shift_ref, q_ref, k_ref, v_ref, o_ref, p_ref, acc_ref):
    nq = q_ref.shape[0]
    n = v_ref.shape[0]
    tk = v_ref.shape[-1]
    total = nq * n
    log2n = n.bit_length() - 1
    unroll = min(n, MLA_MAX_UNROLL)
    shift = shift_ref[0, 0]
    ones = jnp.ones((PV_LHS_ROWS - V_DIM_B, tk), BF16)

    def probs(t, slot):
        c = t & (n - 1)
        k = k_ref[pl.ds(pl.multiple_of(c * tk, tk), tk), :]
        s = jnp.dot(k, q_ref[t >> log2n], preferred_element_type=F32)
        p_ref[slot] = jnp.exp2(s - shift).astype(BF16)

    def pv(t, slot):
        v1 = jnp.concatenate([v_ref[t & (n - 1)], ones], axis=0)
        acc_ref[...] += jnp.dot(v1, p_ref[slot], preferred_element_type=F32)

    acc_ref[...] = jnp.zeros_like(acc_ref)
    probs(0, 0)

    def body(j, carry):
        t0 = j * unroll
        for u in range(unroll):
            probs(jnp.minimum(t0 + u + 1, total - 1), (u + 1) % 2)
            pv(t0 + u, u % 2)

        @pl.when(((t0 + unroll) & (n - 1)) == 0)
        def _():
            acc = acc_ref[...]
            o_ref[(t0 + unroll - 1) >> log2n] = (acc[:V_DIM_B] * (1.0 / acc[V_DIM_B:V_DIM_B + 1])).astype(BF16)
            acc_ref[...] = jnp.zeros_like(acc_ref)

        return carry

    lax.fori_loop(0, total // unroll, body, 0)


def _mla_call(kernel_fn, scratch, qb_t, kb, vb_t, *scalars):
    B, H, nq, _, tq = qb_t.shape
    nck, tk = vb_t.shape[2], vb_t.shape[4]
    S = nck * tk
    assert nck & (nck - 1) == 0 and nck >= 2
    return pl.pallas_call(
        kernel_fn,
        grid=(B, H),
        in_specs=[pl.BlockSpec(memory_space=pltpu.SMEM)] * len(scalars) + [
            pl.BlockSpec((None, None, nq, QK_PAD_B, tq), lambda b, h: (b, h, 0, 0, 0)),
            pl.BlockSpec((None, None, S, QK_PAD_B), lambda b, h: (b, h, 0, 0)),
            pl.BlockSpec((None, None, nck, V_DIM_B, tk), lambda b, h: (b, h, 0, 0, 0)),
        ],
        out_specs=pl.BlockSpec((None, None, nq, V_DIM_B, tq), lambda b, h: (b, h, 0, 0, 0)),
        out_shape=jax.ShapeDtypeStruct((B, H, nq, V_DIM_B, tq), BF16),
        scratch_shapes=scratch(tk, tq),
        compiler_params=_cparams(("arbitrary", "arbitrary")),
        name=kernel_fn.__name__.strip("_").replace("_kernel", ""),
    )(*scalars, qb_t, kb, vb_t)


def _mla_online_scratch(tk, tq):
    return [pltpu.VMEM((1, tq), F32), pltpu.VMEM((V_DIM_B + ONES_ROWS, tq), F32)]


def _mla_bounded_scratch(tk, tq):
    return [pltpu.VMEM((2, tk, tq), BF16), pltpu.VMEM((PV_LHS_ROWS, tq), F32)]


def _mla(qb_t, kb, vb_t, score_bound):
    shift = score_bound.reshape(1, 1).astype(F32)
    return lax.cond(
        2.0 * score_bound <= SOFTMAX_MAX_SHIFT_RANGE,
        lambda: _mla_call(_mla_bounded_kernel, _mla_bounded_scratch, qb_t, kb, vb_t, shift),
        lambda: _mla_call(_mla_online_kernel, _mla_online_scratch, qb_t, kb, vb_t),
    )


def _rope_table(S):
    pos = jnp.arange(S, dtype=F32)

    def cs(d):
        inv = ROPE_THETA ** (-jnp.arange(0, d, 2, dtype=F32) / d)
        ang = pos[:, None] * inv[None, :]
        return jnp.cos(ang).T, jnp.sin(ang).T

    ca, sa = cs(HEAD_DIM_A)
    cb, sb = cs(ROPE_B)
    return jnp.concatenate([ca, sa, cb, sb], axis=0)


def _gain_tile(swa_q_norm, swa_k_norm, mla_q_lora_norm, mla_kv_lora_norm, mla_q_norm, mla_k_norm):
    rows = jnp.concatenate([
        swa_q_norm, swa_k_norm, mla_q_lora_norm, mla_kv_lora_norm,
        mla_q_norm, mla_k_norm[:NOPE_B], mla_k_norm[NOPE_B:],
    ]).astype(F32)
    return jnp.broadcast_to(rows[:, None], (G_ROWS, PROJ_SUB))


def _score_bound(dim, q_scale, q_gain, k_gain):
    return (BF16_ROUND_MARGIN * dim * q_scale * jnp.max(jnp.abs(q_gain)) * jnp.max(jnp.abs(k_gain))).astype(F32)


def kernel(x_prompt, x_sample, c_prompt, c_sample, ada_w, ada_b, ffn1_norm, ffn1_wg, ffn1_wu, ffn1_wd, mix_norm, w_in, swa_q_norm, swa_k_norm, swa_sink, mla_q_lora_norm, mla_w_uq, mla_kv_lora_norm, mla_w_ukv, mla_q_norm, mla_k_norm, w_out, ffn2_norm, ffn2_wg, ffn2_wu, ffn2_wd, final_norm):
    assert ada_w.shape[0] == 1, "single-layer trunk"
    bp, bs = x_prompt.shape[0], x_sample.shape[0]
    assert bp + bs <= MOD_ROWS

    f1 = (ffn1_wg[0].astype(BF16), ffn1_wu[0].astype(BF16), ffn1_wd[0].astype(BF16))
    f2 = (ffn2_wg[0].astype(BF16), ffn2_wu[0].astype(BF16), ffn2_wd[0].astype(BF16))
    w1t = w_in[0].T.astype(BF16)
    wuq = mla_w_uq[0].reshape(Q_LORA, N_HEADS_B, QK_DIM_B)
    wuq = jnp.pad(wuq, ((0, 0), (0, 0), (0, QK_PAD_B - QK_DIM_B))).reshape(Q_LORA, N_HEADS_B * QK_PAD_B)
    wuqt = wuq.T.astype(BF16)
    wukvt = mla_w_ukv[0].T.astype(BF16)
    woa = w_out[0, :A_Q].astype(BF16)
    wob = w_out[0, A_Q:].astype(BF16)
    gains = _gain_tile(swa_q_norm[0], swa_k_norm[0], mla_q_lora_norm[0], mla_kv_lora_norm[0],
                       mla_q_norm[0], mla_k_norm[0])
    sink_rows = jnp.repeat(swa_sink[0].astype(F32) * LOG2E, LANE).reshape(N_KV_A, 1, GROUP_A * LANE)
    row = lambda v: v.reshape(1, D_MODEL)
    bound_a = _score_bound(HEAD_DIM_A, SWA_Q_SCALE, swa_q_norm[0], swa_k_norm[0])
    bound_b = _score_bound(QK_DIM_B, MLA_Q_SCALE, mla_q_norm[0], mla_k_norm[0])

    c_all = jnp.concatenate([c_prompt, c_sample, jnp.zeros((MOD_ROWS - bp - bs, D_MODEL), F32)], axis=0)
    mod = _modulation(c_all, ada_w[0], ada_b[0].reshape(1, -1)).reshape(MOD_ROWS, N_MOD, 1, D_MODEL)

    def trunk(x, m):
        sh1, sc1, g1, sh2, sc2, g2, sh3, sc3, g3 = (m[:, i] for i in range(N_MOD))
        S = x.shape[1]
        x1 = _ffn1(x, sh1, sc1, g1, row(ffn1_norm[0]), *f1)
        qa_t, ka, va_t, qb_t, kb, vb_t = _inproj(x1, sh2, sc2, row(mix_norm[0]), w1t, wuqt, wukvt,
                                                  gains, _rope_table(S))
        oa_t = _swa(qa_t, ka, va_t, sink_rows, bound_a)
        ob_t = _mla(qb_t, kb, vb_t, bound_b)
        return _ffn2(x1, oa_t.reshape(-1, A_Q, S), ob_t, woa, wob, g2, sh3, sc3, g3,
                     row(ffn2_norm[0]), *f2, row(final_norm[0]))

    y_prompt = trunk(x_prompt, mod[:bp])
    y_sample = trunk(x_sample, mod[bp:bp + bs])
    return (y_prompt, y_sample)
```

```python
import functools
import math

import jax
import jax.numpy as jnp
from jax import lax
from jax.experimental import pallas as pl
from jax.experimental.pallas import tpu as pltpu

F32 = jnp.float32
BF16 = jnp.bfloat16

D_MODEL = 1024
D_FF = 2816
N_HEADS_A = 8
N_KV_A = 2
HEAD_DIM_A = 64
GROUP_A = N_HEADS_A // N_KV_A
WINDOW = 128
N_HEADS_B = 8
Q_LORA = 256
KV_LORA = 128
NOPE_B = 64
ROPE_B = 32
QK_DIM_B = NOPE_B + ROPE_B
V_DIM_B = 64
ROPE_THETA = 10000.0
EPS = 1e-6
N_MOD = 9
A_Q = N_HEADS_A * HEAD_DIM_A
A_KV = N_KV_A * HEAD_DIM_A
IN_WIDTH = A_Q + 2 * A_KV + Q_LORA + KV_LORA + ROPE_B
NEG_BIG = -1e30
LOG2E = math.log2(math.e)
SWA_Q_SCALE = LOG2E / math.sqrt(HEAD_DIM_A)
MLA_Q_SCALE = LOG2E / math.sqrt(QK_DIM_B)
SOFTMAX_MAX_SHIFT_RANGE = 100.0
BF16_ROUND_MARGIN = 1.02

LANE = 128
QK_PAD_B = LANE
VMEM_LIMIT_BYTES = 56 * 1024 * 1024

MOD_ROWS = 16
MOD_BN = 1152
FFN_TM = 512
FFN_FC = 256
FFN_NC = D_FF // FFN_FC
PROJ_TM = 1024
PROJ_SUB = 512
SWA_TQ = 512
MLA_TQ = 512
MLA_MAX_UNROLL = 16
MLA_TK = 512
ONES_ROWS = 16
PV_LHS_ROWS = 128

R_QA, R_KA, R_VA = 0, A_Q, A_Q + A_KV
R_CQ = A_Q + 2 * A_KV
R_CKV = R_CQ + Q_LORA
R_KR = R_CKV + KV_LORA
G_QA, G_KA = 0, HEAD_DIM_A
G_QL = 2 * HEAD_DIM_A
G_KVL = G_QL + Q_LORA
G_MQ = G_KVL + KV_LORA
G_KN = G_MQ + QK_DIM_B
G_KR = G_KN + NOPE_B
G_ROWS = G_KR + ROPE_B
T_CA, T_SA = 0, HEAD_DIM_A // 2
T_CB = HEAD_DIM_A
T_SB = T_CB + ROPE_B // 2
T_ROWS = T_SB + ROPE_B // 2

_NT = (((1,), (1,)), ((), ()))
_TN = (((0,), (0,)), ((), ()))


def _cparams(sem):
    return pltpu.CompilerParams(dimension_semantics=sem, vmem_limit_bytes=VMEM_LIMIT_BYTES)


def _const_spec(shape):
    n = len(shape)
    return pl.BlockSpec(shape, lambda *_: (0,) * n, pipeline_mode=pl.Buffered(1))


def _silu(x):
    return x * (1.0 / (1.0 + jnp.exp(-x)))


def _rms_rows(x):
    return lax.rsqrt(jnp.mean(x * x, axis=-1, keepdims=True) + EPS)


def _mod_kernel(c_ref, w_ref, b_ref, o_ref):
    a = _silu(c_ref[...]).astype(BF16)
    o_ref[...] = jnp.dot(a, w_ref[...].astype(BF16), preferred_element_type=F32) + b_ref[...]


def _modulation(c, ada_w, ada_b):
    n = ada_w.shape[1]
    return pl.pallas_call(
        _mod_kernel,
        grid=(n // MOD_BN,),
        in_specs=[
            pl.BlockSpec((MOD_ROWS, D_MODEL), lambda j: (0, 0)),
            pl.BlockSpec((D_MODEL, MOD_BN), lambda j: (0, j)),
            pl.BlockSpec((1, MOD_BN), lambda j: (0, j)),
        ],
        out_specs=pl.BlockSpec((MOD_ROWS, MOD_BN), lambda j: (0, j)),
        out_shape=jax.ShapeDtypeStruct((MOD_ROWS, n), F32),
        compiler_params=_cparams(("arbitrary",)),
        name="mod",
    )(c, ada_w, ada_b)


def _swiglu_into(acc_ref, h_ref, wg_ref, wu_ref, wd_ref):
    for c in range(FFN_NC):
        cols = slice(c * FFN_FC, (c + 1) * FFN_FC)
        h = h_ref[...]
        g = jnp.dot(h, wg_ref[:, cols], preferred_element_type=F32)
        u = jnp.dot(h, wu_ref[:, cols], preferred_element_type=F32)
        a = (_silu(g) * u).astype(BF16)
        y = jnp.dot(a, wd_ref[cols, :], preferred_element_type=F32)
        if c == 0:
            acc_ref[...] = y
        else:
            acc_ref[...] += y


def _ffn1_kernel(x_ref, sh_ref, sc_ref, gt_ref, nw_ref, wg_ref, wu_ref, wd_ref, o_ref, h_ref, acc_ref):
    x = x_ref[...]
    h_ref[...] = (x * _rms_rows(x) * (nw_ref[...] * (1.0 + sc_ref[...])) + sh_ref[...]).astype(BF16)
    _swiglu_into(acc_ref, h_ref, wg_ref, wu_ref, wd_ref)
    o_ref[...] = x_ref[...] + (0.5 * gt_ref[...]) * acc_ref[...]


def _ffn2_kernel(x_ref, oa_ref, ob_ref, woa_ref, wob_ref, g2_ref, sh_ref, sc_ref, gt_ref, nw_ref,
                 wg_ref, wu_ref, wd_ref, fw_ref, o_ref, h_ref, acc_ref, x2_ref):
    y = lax.dot_general(oa_ref[...], woa_ref[...], _TN, preferred_element_type=F32)
    ob = ob_ref[...].reshape(N_HEADS_B * V_DIM_B, ob_ref.shape[-1])
    y = y + lax.dot_general(ob, wob_ref[...], _TN, preferred_element_type=F32)
    x = x_ref[...] + g2_ref[...] * y
    x2_ref[...] = x
    h_ref[...] = (x * _rms_rows(x) * (nw_ref[...] * (1.0 + sc_ref[...])) + sh_ref[...]).astype(BF16)
    _swiglu_into(acc_ref, h_ref, wg_ref, wu_ref, wd_ref)
    z = x2_ref[...] + (0.5 * gt_ref[...]) * acc_ref[...]
    o_ref[...] = z * _rms_rows(z) * fw_ref[...]


def _row_spec():
    return pl.BlockSpec((None, 1, D_MODEL), lambda b, t: (b, 0, 0))


def _ffn_weight_specs():
    return [_const_spec((D_MODEL, D_FF)), _const_spec((D_MODEL, D_FF)), _const_spec((D_FF, D_MODEL))]


def _ffn1(x, sh, sc, gt, nw, wg, wu, wd):
    B, S, _ = x.shape
    tm = FFN_TM
    tok = pl.BlockSpec((None, tm, D_MODEL), lambda b, t: (b, t, 0))
    return pl.pallas_call(
        _ffn1_kernel,
        grid=(B, S // tm),
        in_specs=[tok, _row_spec(), _row_spec(), _row_spec(), _const_spec((1, D_MODEL))] + _ffn_weight_specs(),
        out_specs=tok,
        out_shape=jax.ShapeDtypeStruct(x.shape, F32),
        scratch_shapes=[pltpu.VMEM((tm, D_MODEL), BF16), pltpu.VMEM((tm, D_MODEL), F32)],
        compiler_params=_cparams(("arbitrary", "arbitrary")),
        name="ffn1",
    )(x, sh, sc, gt, nw, wg, wu, wd)


def _ffn2(x, oa_t, ob_t, woa, wob, g2, sh, sc, gt, nw, wg, wu, wd, fw):
    B, S, _ = x.shape
    tm = FFN_TM
    assert tm == MLA_TQ
    tok = pl.BlockSpec((None, tm, D_MODEL), lambda b, t: (b, t, 0))
    att_a = pl.BlockSpec((None, A_Q, tm), lambda b, t: (b, 0, t))
    att_b = pl.BlockSpec((None, N_HEADS_B, None, V_DIM_B, tm), lambda b, t: (b, 0, t, 0, 0))
    return pl.pallas_call(
        _ffn2_kernel,
        grid=(B, S // tm),
        in_specs=[tok, att_a, att_b, _const_spec((A_Q, D_MODEL)), _const_spec((A_Q, D_MODEL)),
                  _row_spec(), _row_spec(), _row_spec(), _row_spec(), _const_spec((1, D_MODEL))]
                 + _ffn_weight_specs() + [_const_spec((1, D_MODEL))],
        out_specs=tok,
        out_shape=jax.ShapeDtypeStruct(x.shape, F32),
        scratch_shapes=[pltpu.VMEM((tm, D_MODEL), BF16), pltpu.VMEM((tm, D_MODEL), F32),
                        pltpu.VMEM((tm, D_MODEL), F32)],
        compiler_params=_cparams(("arbitrary", "arbitrary")),
        name="ffn2",
    )(x, oa_t, ob_t, woa, wob, g2, sh, sc, gt, nw, wg, wu, wd, fw)


def _rms_cols(x, n):
    return lax.rsqrt(jnp.sum(x * x, axis=0, keepdims=True) * (1.0 / n) + EPS)


def _rope_cols(y, cos, sin):
    half = y.shape[0] // 2
    y1, y2 = y[:half], y[half:]
    return jnp.concatenate([y1 * cos - y2 * sin, y2 * cos + y1 * sin], axis=0)


def _inproj_kernel(x_ref, sh_ref, sc_ref, nw_ref, w1_ref, wuq_ref, wukv_ref, g_ref, t_ref,
                   qa_ref, ka_ref, va_ref, qb_ref, kb_ref, vb_ref,
                   zt_ref, cq_ref, ckv_ref, qbt_ref, kvt_ref):
    n = PROJ_SUB

    def tables(tok):
        return (t_ref[T_CA:T_CA + 32, tok], t_ref[T_SA:T_SA + 32, tok],
                t_ref[T_CB:T_CB + 16, tok], t_ref[T_SB:T_SB + 16, tok])

    def project(sub):
        x = x_ref[sub * n:(sub + 1) * n, :]
        h = (x * _rms_rows(x) * (nw_ref[...] * (1.0 + sc_ref[...])) + sh_ref[...]).astype(BF16)
        zt_ref[sub] = lax.dot_general(w1_ref[...], h, _NT, preferred_element_type=F32)

    def swa_heads(sub):
        off = sub * n
        tok = slice(off, off + n)
        cos_a, sin_a, _, _ = tables(tok)
        zt = zt_ref.at[sub]
        g_qa = g_ref[G_QA:G_QA + HEAD_DIM_A, :]
        for hd in range(N_HEADS_A):
            blk = zt[R_QA + hd * HEAD_DIM_A:R_QA + (hd + 1) * HEAD_DIM_A, :]
            r = _rms_cols(blk, HEAD_DIM_A) * SWA_Q_SCALE
            qa_ref[hd, :, tok] = (_rope_cols(blk * g_qa, cos_a, sin_a) * r).astype(BF16)
        g_ka = g_ref[G_KA:G_KA + HEAD_DIM_A, :]
        ks = []
        for hd in range(N_KV_A):
            blk = zt[R_KA + hd * HEAD_DIM_A:R_KA + (hd + 1) * HEAD_DIM_A, :]
            ks.append(_rope_cols(blk * g_ka, cos_a, sin_a) * _rms_cols(blk, HEAD_DIM_A))
        ka_ref[tok, :] = jnp.concatenate(ks, axis=0).T.astype(BF16)
        vt = zt[R_VA:R_VA + A_KV, :].astype(BF16)
        for c in range(n // LANE):
            va_ref[off // LANE + c] = vt[:, c * LANE:(c + 1) * LANE]
        cq = zt[R_CQ:R_CQ + Q_LORA, :]
        cq_ref[sub] = (cq * g_ref[G_QL:G_QL + Q_LORA, :] * _rms_cols(cq, Q_LORA)).astype(BF16)
        ckv = zt[R_CKV:R_CKV + KV_LORA, :]
        ckv_ref[sub] = (ckv * g_ref[G_KVL:G_KVL + KV_LORA, :] * _rms_cols(ckv, KV_LORA)).astype(BF16)

    def up_project(sub):
        qbt_ref[sub] = jnp.dot(wuq_ref[...], cq_ref[sub], preferred_element_type=F32)
        kvt_ref[sub] = jnp.dot(wukv_ref[...], ckv_ref[sub], preferred_element_type=F32)

    def mla_heads(sub):
        off = sub * n
        tok = slice(off, off + n)
        _, _, cos_b, sin_b = tables(tok)
        g_mq = g_ref[G_MQ:G_MQ + QK_DIM_B, :]
        pad = jnp.zeros((QK_PAD_B - QK_DIM_B, n), BF16)
        for hd in range(N_HEADS_B):
            blk = qbt_ref[sub, hd * QK_PAD_B:hd * QK_PAD_B + QK_DIM_B, :]
            y = blk * g_mq
            o = jnp.concatenate([y[:NOPE_B], _rope_cols(y[NOPE_B:], cos_b, sin_b)], axis=0)
            o = (o * (_rms_cols(blk, QK_DIM_B) * MLA_Q_SCALE)).astype(BF16)
            step = min(n, MLA_TQ)
            for c in range(n // step):
                lo = off + c * step
                qb_ref[hd, lo // MLA_TQ, :, lo % MLA_TQ:lo % MLA_TQ + step] = (
                    jnp.concatenate([o[:, c * step:(c + 1) * step], pad[:, :step]], axis=0))
        kr = zt_ref[sub, R_KR:R_KR + ROPE_B, :]
        ss_kr = jnp.sum(kr * kr, axis=0, keepdims=True)
        kr_rot = _rope_cols(kr * g_ref[G_KR:G_KR + ROPE_B, :], cos_b, sin_b)
        g_kn = g_ref[G_KN:G_KN + NOPE_B, :]
        padf = jnp.zeros((QK_PAD_B - QK_DIM_B, n), F32)
        for hd in range(N_HEADS_B):
            kn = kvt_ref[sub, hd * 2 * NOPE_B:hd * 2 * NOPE_B + NOPE_B, :]
            r = lax.rsqrt((jnp.sum(kn * kn, axis=0, keepdims=True) + ss_kr) * (1.0 / QK_DIM_B) + EPS)
            kt = jnp.concatenate([kn * g_kn * r, kr_rot * r, padf], axis=0)
            kb_ref[hd, tok, :] = kt.T.astype(BF16)
            v = kvt_ref[sub, hd * 2 * NOPE_B + NOPE_B:(hd + 1) * 2 * NOPE_B, :].astype(BF16)
            step = min(n, MLA_TK)
            for c in range(n // step):
                lo = off + c * step
                vb_ref[hd, lo // MLA_TK, :, lo % MLA_TK:lo % MLA_TK + step] = v[:, c * step:(c + 1) * step]

    assert PROJ_TM == 2 * PROJ_SUB
    project(0)
    project(1)
    swa_heads(0)
    up_project(0)
    swa_heads(1)
    mla_heads(0)
    up_project(1)
    mla_heads(1)


def _inproj(x, sh, sc, nw, w1t, wuqt, wukvt, gains, table):
    B, S, _ = x.shape
    tm = PROJ_TM
    nt = S // tm
    out_shape = [
        jax.ShapeDtypeStruct((B, N_HEADS_A, HEAD_DIM_A, S), BF16),
        jax.ShapeDtypeStruct((B, S, A_KV), BF16),
        jax.ShapeDtypeStruct((B, S // LANE, A_KV, LANE), BF16),
        jax.ShapeDtypeStruct((B, N_HEADS_B, S // MLA_TQ, QK_PAD_B, MLA_TQ), BF16),
        jax.ShapeDtypeStruct((B, N_HEADS_B, S, QK_PAD_B), BF16),
        jax.ShapeDtypeStruct((B, N_HEADS_B, S // MLA_TK, V_DIM_B, MLA_TK), BF16),
    ]
    out_specs = [
        pl.BlockSpec((None, N_HEADS_A, HEAD_DIM_A, tm), lambda b, t: (b, 0, 0, t)),
        pl.BlockSpec((None, tm, A_KV), lambda b, t: (b, t, 0)),
        pl.BlockSpec((None, tm // LANE, A_KV, LANE), lambda b, t: (b, t, 0, 0)),
        pl.BlockSpec((None, N_HEADS_B, tm // MLA_TQ, QK_PAD_B, MLA_TQ), lambda b, t: (b, 0, t, 0, 0)),
        pl.BlockSpec((None, N_HEADS_B, tm, QK_PAD_B), lambda b, t: (b, 0, t, 0)),
        pl.BlockSpec((None, N_HEADS_B, tm // MLA_TK, V_DIM_B, MLA_TK), lambda b, t: (b, 0, t, 0, 0)),
    ]
    return pl.pallas_call(
        _inproj_kernel,
        grid=(B, nt),
        in_specs=[
            pl.BlockSpec((None, tm, D_MODEL), lambda b, t: (b, t, 0)),
            _row_spec(), _row_spec(), _const_spec((1, D_MODEL)),
            _const_spec((IN_WIDTH, D_MODEL)),
            _const_spec((N_HEADS_B * QK_PAD_B, Q_LORA)),
            _const_spec((N_HEADS_B * 2 * NOPE_B, KV_LORA)),
            _const_spec((G_ROWS, PROJ_SUB)),
            pl.BlockSpec((T_ROWS, tm), lambda b, t: (0, t)),
        ],
        out_specs=out_specs,
        out_shape=out_shape,
        scratch_shapes=[
            pltpu.VMEM((2, IN_WIDTH, PROJ_SUB), F32),
            pltpu.VMEM((2, Q_LORA, PROJ_SUB), BF16),
            pltpu.VMEM((2, KV_LORA, PROJ_SUB), BF16),
            pltpu.VMEM((2, N_HEADS_B * QK_PAD_B, PROJ_SUB), F32),
            pltpu.VMEM((2, N_HEADS_B * 2 * NOPE_B, PROJ_SUB), F32),
        ],
        compiler_params=_cparams(("arbitrary", "arbitrary")),
        name="inproj",
    )(x, sh, sc, nw, w1t, wuqt, wukvt, gains, table)


def _swa_kernel(*refs, n_blocks, bounded):
    if bounded:
        shift_ref, q_ref, k_ref, v_ref, sink_ref, o_ref, p_ref = refs
    else:
        q_ref, k_ref, v_ref, sink_ref, o_ref, p_ref, m_ref = refs
    tq = q_ref.shape[-1]
    nq = tq // LANE
    wide = GROUP_A * LANE
    row = lax.broadcasted_iota(jnp.int32, (LANE, wide), 0)
    col = lax.broadcasted_iota(jnp.int32, (LANE, wide), 1) & (LANE - 1)
    diff = row - col
    zeros = jnp.zeros((HEAD_DIM_A, LANE), BF16)
    ones = jnp.ones((ONES_ROWS, LANE), BF16)
    chains = [(i, j) for i in range(nq) for j in range(N_KV_A)]

    def blocks(i):
        nb = pl.program_id(1) * nq + i
        kb_idx = [jnp.maximum(nb - 1, 0), nb, jnp.minimum(nb + 1, n_blocks - 1)]
        lo = jnp.where(nb > 0, 0, 2 * LANE)
        hi = jnp.where(nb < n_blocks - 1, 0, -2 * LANE)
        return kb_idx, [diff >= lo, None, diff <= hi]

    def probs(i, j, slot):
        kb_idx, masks = blocks(i)
        cols = []
        for g in range(GROUP_A):
            qh = q_ref[j * GROUP_A + g, :, i * LANE:(i + 1) * LANE]
            cols.append(jnp.concatenate([qh, zeros] if j == 0 else [zeros, qh], axis=0))
        qp = jnp.concatenate(cols, axis=1)
        ss = []
        for c in range(3):
            k = k_ref[pl.ds(pl.multiple_of(kb_idx[c] * LANE, LANE), LANE), :]
            ss.append(jnp.dot(k, qp, preferred_element_type=F32))
        if bounded:
            m = shift_ref[0, 0]
        else:
            m = sink_ref[j]
            for c in range(3):
                sc = ss[c] if masks[c] is None else jnp.where(masks[c], ss[c], NEG_BIG)
                m = jnp.maximum(m, jnp.max(sc, axis=0, keepdims=True))
            m_ref[slot] = m
        for c in range(3):
            p = jnp.exp2(ss[c] - m)
            if masks[c] is not None:
                p = jnp.where(masks[c], p, 0.0)
            p_ref[slot, c] = p.astype(BF16)

    def apply(i, j, slot):
        kb_idx, _ = blocks(i)
        m = shift_ref[0, 0] if bounded else m_ref[slot]
        acc = None
        for c in range(3):
            vt = v_ref[kb_idx[c], j * HEAD_DIM_A:(j + 1) * HEAD_DIM_A, :]
            pv = jnp.dot(jnp.concatenate([vt, ones], axis=0), p_ref[slot, c], preferred_element_type=F32)
            acc = pv if acc is None else acc + pv
        l = acc[HEAD_DIM_A:HEAD_DIM_A + 1] + jnp.exp2(sink_ref[j] - m)
        o = acc[:HEAD_DIM_A] * (1.0 / l)
        for g in range(GROUP_A):
            o_ref[j * GROUP_A + g, :, i * LANE:(i + 1) * LANE] = o[:, g * LANE:(g + 1) * LANE].astype(BF16)

    probs(*chains[0], 0)
    for c, (i, j) in enumerate(chains):
        if c + 1 < len(chains):
            probs(*chains[c + 1], (c + 1) % 2)
        apply(i, j, c % 2)


def _swa_call(bounded, qa_t, ka, va_t, sink_rows, *scalars):
    B, _, _, S = qa_t.shape
    tq = SWA_TQ
    nbk = S // LANE
    qspec = pl.BlockSpec((None, N_HEADS_A, HEAD_DIM_A, tq), lambda b, t: (b, 0, 0, t))
    return pl.pallas_call(
        functools.partial(_swa_kernel, n_blocks=nbk, bounded=bounded),
        grid=(B, S // tq),
        in_specs=[pl.BlockSpec(memory_space=pltpu.SMEM)] * len(scalars) + [
            qspec,
            pl.BlockSpec((None, S, A_KV), lambda b, t: (b, 0, 0)),
            pl.BlockSpec((None, nbk, A_KV, LANE), lambda b, t: (b, 0, 0, 0)),
            _const_spec((N_KV_A, 1, GROUP_A * LANE)),
        ],
        out_specs=qspec,
        out_shape=jax.ShapeDtypeStruct(qa_t.shape, BF16),
        scratch_shapes=[pltpu.VMEM((2, 3, LANE, GROUP_A * LANE), BF16)]
                       + ([] if bounded else [pltpu.VMEM((2, 1, GROUP_A * LANE), F32)]),
        compiler_params=_cparams(("arbitrary", "arbitrary")),
        name="swa_bounded" if bounded else "swa_online",
    )(*scalars, qa_t, ka, va_t, sink_rows)


def _swa(qa_t, ka, va_t, sink_rows, score_bound):
    shift = jnp.maximum(score_bound, jnp.max(sink_rows))
    return lax.cond(
        shift + score_bound <= SOFTMAX_MAX_SHIFT_RANGE,
        lambda: _swa_call(True, qa_t, ka, va_t, sink_rows, shift.reshape(1, 1).astype(F32)),
        lambda: _swa_call(False, qa_t, ka, va_t, sink_rows),
    )


def _mla_online_kernel(q_ref, k_ref, v_ref, o_ref, m_ref, acc_ref):
    nq = q_ref.shape[0]
    n = v_ref.shape[0]
    tk = v_ref.shape[-1]
    ones = jnp.ones((ONES_ROWS, tk), BF16)

    def qblock(qb, carry):
        m_ref[...] = jnp.full_like(m_ref, NEG_BIG)
        acc_ref[...] = jnp.zeros_like(acc_ref)

        def chunk(c, carry2):
            k = k_ref[pl.ds(pl.multiple_of(c * tk, tk), tk), :]
            s = jnp.dot(k, q_ref[qb], preferred_element_type=F32)
            m_old = m_ref[...]
            m_new = jnp.maximum(m_old, jnp.max(s, axis=0, keepdims=True))
            p = jnp.exp2(s - m_new).astype(BF16)
            v1 = jnp.concatenate([v_ref[c], ones], axis=0)
            acc_ref[...] = jnp.exp2(m_old - m_new) * acc_ref[...] + jnp.dot(v1, p, preferred_element_type=F32)
            m_ref[...] = m_new
            return carry2

        lax.fori_loop(0, n, chunk, 0)
        acc = acc_ref[...]
        o_ref[qb] = (acc[:V_DIM_B] * (1.0 / acc[V_DIM_B:V_DIM_B + 1])).astype(BF16)
        return carry

    lax.fori_loop(0, nq, qblock, 0)


def _mla_bounded_kernel(shift_ref, q_ref, k_ref, v_ref, o_ref, p_ref):
    nq = q_ref.shape[0]
    n = v_ref.shape[0]
    tk = v_ref.shape[-1]
    total = nq * n
    log2n = n.bit_length() - 1
    qpb = min(nq, max(1, MLA_MAX_UNROLL // n))
    shift = shift_ref[0, 0]
    ones = jnp.ones((PV_LHS_ROWS - V_DIM_B, tk), BF16)

    def probs(t, slot):
        c = t & (n - 1)
        k = k_ref[pl.ds(pl.multiple_of(c * tk, tk), tk), :]
        s = jnp.dot(k, q_ref[t >> log2n], preferred_element_type=F32)
        p_ref[slot] = jnp.exp2(s - shift).astype(BF16)

    def pv(c, slot):
        v1 = jnp.concatenate([v_ref[c], ones], axis=0)
        return jnp.dot(v1, p_ref[slot], preferred_element_type=F32)

    probs(0, 0)

    def body(j, carry):
        t0 = j * (qpb * n)
        for u in range(qpb * n):
            probs(jnp.minimum(t0 + u + 1, total - 1), (u + 1) % 2)
            part = pv(u % n, u % 2)
            acc = part if u % n == 0 else acc + part
            if u % n == n - 1:
                o_ref[j * qpb + u // n] = (acc[:V_DIM_B] * (1.0 / acc[V_DIM_B:V_DIM_B + 1])).astype(BF16)
        return carry

    lax.fori_loop(0, nq // qpb, body, 0)


def _mla_call(kernel_fn, scratch, qb_t, kb, vb_t, *scalars):
    B, H, nq, _, tq = qb_t.shape
    nck, tk = vb_t.shape[2], vb_t.shape[4]
    S = nck * tk
    assert nck & (nck - 1) == 0 and nq & (nq - 1) == 0 and nck >= 2
    return pl.pallas_call(
        kernel_fn,
        grid=(B, H),
        in_specs=[pl.BlockSpec(memory_space=pltpu.SMEM)] * len(scalars) + [
            pl.BlockSpec((None, None, nq, QK_PAD_B, tq), lambda b, h: (b, h, 0, 0, 0)),
            pl.BlockSpec((None, None, S, QK_PAD_B), lambda b, h: (b, h, 0, 0)),
            pl.BlockSpec((None, None, nck, V_DIM_B, tk), lambda b, h: (b, h, 0, 0, 0)),
        ],
        out_specs=pl.BlockSpec((None, None, nq, V_DIM_B, tq), lambda b, h: (b, h, 0, 0, 0)),
        out_shape=jax.ShapeDtypeStruct((B, H, nq, V_DIM_B, tq), BF16),
        scratch_shapes=scratch(tk, tq),
        compiler_params=_cparams(("arbitrary", "arbitrary")),
        name=kernel_fn.__name__.strip("_").replace("_kernel", ""),
    )(*scalars, qb_t, kb, vb_t)


def _mla_online_scratch(tk, tq):
    return [pltpu.VMEM((1, tq), F32), pltpu.VMEM((V_DIM_B + ONES_ROWS, tq), F32)]


def _mla_bounded_scratch(tk, tq):
    return [pltpu.VMEM((2, tk, tq), BF16)]


def _mla(qb_t, kb, vb_t, score_bound):
    shift = score_bound.reshape(1, 1).astype(F32)
    return lax.cond(
        2.0 * score_bound <= SOFTMAX_MAX_SHIFT_RANGE,
        lambda: _mla_call(_mla_bounded_kernel, _mla_bounded_scratch, qb_t, kb, vb_t, shift),
        lambda: _mla_call(_mla_online_kernel, _mla_online_scratch, qb_t, kb, vb_t),
    )


def _rope_table(S):
    pos = jnp.arange(S, dtype=F32)

    def cs(d):
        inv = ROPE_THETA ** (-jnp.arange(0, d, 2, dtype=F32) / d)
        ang = pos[:, None] * inv[None, :]
        return jnp.cos(ang).T, jnp.sin(ang).T

    ca, sa = cs(HEAD_DIM_A)
    cb, sb = cs(ROPE_B)
    return jnp.concatenate([ca, sa, cb, sb], axis=0)


def _gain_tile(swa_q_norm, swa_k_norm, mla_q_lora_norm, mla_kv_lora_norm, mla_q_norm, mla_k_norm):
    rows = jnp.concatenate([
        swa_q_norm, swa_k_norm, mla_q_lora_norm, mla_kv_lora_norm,
        mla_q_norm, mla_k_norm[:NOPE_B], mla_k_norm[NOPE_B:],
    ]).astype(F32)
    return jnp.broadcast_to(rows[:, None], (G_ROWS, PROJ_SUB))


def _score_bound(dim, q_scale, q_gain, k_gain):
    return (BF16_ROUND_MARGIN * dim * q_scale * jnp.max(jnp.abs(q_gain)) * jnp.max(jnp.abs(k_gain))).astype(F32)


def kernel(x_prompt, x_sample, c_prompt, c_sample, ada_w, ada_b, ffn1_norm, ffn1_wg, ffn1_wu, ffn1_wd, mix_norm, w_in, swa_q_norm, swa_k_norm, swa_sink, mla_q_lora_norm, mla_w_uq, mla_kv_lora_norm, mla_w_ukv, mla_q_norm, mla_k_norm, w_out, ffn2_norm, ffn2_wg, ffn2_wu, ffn2_wd, final_norm):
    assert ada_w.shape[0] == 1, "single-layer trunk"
    bp, bs = x_prompt.shape[0], x_sample.shape[0]
    assert bp + bs <= MOD_ROWS

    f1 = (ffn1_wg[0].astype(BF16), ffn1_wu[0].astype(BF16), ffn1_wd[0].astype(BF16))
    f2 = (ffn2_wg[0].astype(BF16), ffn2_wu[0].astype(BF16), ffn2_wd[0].astype(BF16))
    w1t = w_in[0].T.astype(BF16)
    wuq = mla_w_uq[0].reshape(Q_LORA, N_HEADS_B, QK_DIM_B)
    wuq = jnp.pad(wuq, ((0, 0), (0, 0), (0, QK_PAD_B - QK_DIM_B))).reshape(Q_LORA, N_HEADS_B * QK_PAD_B)
    wuqt = wuq.T.astype(BF16)
    wukvt = mla_w_ukv[0].T.astype(BF16)
    woa = w_out[0, :A_Q].astype(BF16)
    wob = w_out[0, A_Q:].astype(BF16)
    gains = _gain_tile(swa_q_norm[0], swa_k_norm[0], mla_q_lora_norm[0], mla_kv_lora_norm[0],
                       mla_q_norm[0], mla_k_norm[0])
    sink_rows = jnp.repeat(swa_sink[0].astype(F32) * LOG2E, LANE).reshape(N_KV_A, 1, GROUP_A * LANE)
    row = lambda v: v.reshape(1, D_MODEL)
    bound_a = _score_bound(HEAD_DIM_A, SWA_Q_SCALE, swa_q_norm[0], swa_k_norm[0])
    bound_b = _score_bound(QK_DIM_B, MLA_Q_SCALE, mla_q_norm[0], mla_k_norm[0])

    c_all = jnp.concatenate([c_prompt, c_sample, jnp.zeros((MOD_ROWS - bp - bs, D_MODEL), F32)], axis=0)
    mod = _modulation(c_all, ada_w[0], ada_b[0].reshape(1, -1)).reshape(MOD_ROWS, N_MOD, 1, D_MODEL)

    def trunk(x, m):
        sh1, sc1, g1, sh2, sc2, g2, sh3, sc3, g3 = (m[:, i] for i in range(N_MOD))
        S = x.shape[1]
        x1 = _ffn1(x, sh1, sc1, g1, row(ffn1_norm[0]), *f1)
        qa_t, ka, va_t, qb_t, kb, vb_t = _inproj(x1, sh2, sc2, row(mix_norm[0]), w1t, wuqt, wukvt,
                                                  gains, _rope_table(S))
        oa_t = _swa(qa_t, ka, va_t, sink_rows, bound_a)
        ob_t = _mla(qb_t, kb, vb_t, bound_b)
        return _ffn2(x1, oa_t.reshape(-1, A_Q, S), ob_t, woa, wob, g2, sh3, sc3, g3,
                     row(ffn2_norm[0]), *f2, row(final_norm[0]))

    y_prompt = trunk(x_prompt, mod[:bp])
    y_sample = trunk(x_sample, mod[bp:bp + bs])
    return (y_prompt, y_sample)
```

```python
import functools
import math

import jax
import jax.numpy as jnp
from jax import lax
from jax.experimental import pallas as pl
from jax.experimental.pallas import tpu as pltpu

F32 = jnp.float32
BF16 = jnp.bfloat16

D_MODEL = 1024
D_FF = 2816
N_HEADS_A = 8
N_KV_A = 2
HEAD_DIM_A = 64
GROUP_A = N_HEADS_A // N_KV_A
WINDOW = 128
N_HEADS_B = 8
Q_LORA = 256
KV_LORA = 128
NOPE_B = 64
ROPE_B = 32
QK_DIM_B = NOPE_B + ROPE_B
V_DIM_B = 64
ROPE_THETA = 10000.0
EPS = 1e-6
N_MOD = 9
A_Q = N_HEADS_A * HEAD_DIM_A
A_KV = N_KV_A * HEAD_DIM_A
IN_WIDTH = A_Q + 2 * A_KV + Q_LORA + KV_LORA + ROPE_B
NEG_BIG = -1e30
LOG2E = math.log2(math.e)
SWA_Q_SCALE = LOG2E / math.sqrt(HEAD_DIM_A)
MLA_Q_SCALE = LOG2E / math.sqrt(QK_DIM_B)
SOFTMAX_MAX_SHIFT_RANGE = 100.0
BF16_ROUND_MARGIN = 1.02

LANE = 128
QK_PAD_B = LANE
VMEM_LIMIT_BYTES = 56 * 1024 * 1024

MOD_ROWS = 16
MOD_BN = 1152
FFN_TM = 1024
FFN_FC = 256
FFN_NC = D_FF // FFN_FC
PROJ_TM = 1024
PROJ_SUB = 512
SWA_TQ = 1024
MLA_TQ = 512
MLA_MAX_UNROLL = 16
MLA_TK = 1024
ONES_ROWS = 16
PV_LHS_ROWS = 128

R_QA, R_KA, R_VA = 0, A_Q, A_Q + A_KV
R_CQ = A_Q + 2 * A_KV
R_CKV = R_CQ + Q_LORA
R_KR = R_CKV + KV_LORA
G_QA, G_KA = 0, HEAD_DIM_A
G_QL = 2 * HEAD_DIM_A
G_KVL = G_QL + Q_LORA
G_MQ = G_KVL + KV_LORA
G_KN = G_MQ + QK_DIM_B
G_KR = G_KN + NOPE_B
G_ROWS = G_KR + ROPE_B
T_CA, T_SA = 0, HEAD_DIM_A // 2
T_CB = HEAD_DIM_A
T_SB = T_CB + ROPE_B // 2
T_ROWS = T_SB + ROPE_B // 2

_NT = (((1,), (1,)), ((), ()))
_TN = (((0,), (0,)), ((), ()))


def _cparams(sem):
    return pltpu.CompilerParams(dimension_semantics=sem, vmem_limit_bytes=VMEM_LIMIT_BYTES)


def _const_spec(shape):
    n = len(shape)
    return pl.BlockSpec(shape, lambda *_: (0,) * n, pipeline_mode=pl.Buffered(1))


def _silu(x):
    return x * (1.0 / (1.0 + jnp.exp(-x)))


def _rms_rows(x):
    return lax.rsqrt(jnp.mean(x * x, axis=-1, keepdims=True) + EPS)


def _mod_kernel(c_ref, w_ref, b_ref, o_ref):
    a = _silu(c_ref[...]).astype(BF16)
    o_ref[...] = jnp.dot(a, w_ref[...].astype(BF16), preferred_element_type=F32) + b_ref[...]


def _modulation(c, ada_w, ada_b):
    n = ada_w.shape[1]
    return pl.pallas_call(
        _mod_kernel,
        grid=(n // MOD_BN,),
        in_specs=[
            pl.BlockSpec((MOD_ROWS, D_MODEL), lambda j: (0, 0)),
            pl.BlockSpec((D_MODEL, MOD_BN), lambda j: (0, j)),
            pl.BlockSpec((1, MOD_BN), lambda j: (0, j)),
        ],
        out_specs=pl.BlockSpec((MOD_ROWS, MOD_BN), lambda j: (0, j)),
        out_shape=jax.ShapeDtypeStruct((MOD_ROWS, n), F32),
        compiler_params=_cparams(("arbitrary",)),
        name="mod",
    )(c, ada_w, ada_b)


def _swiglu_into(acc_ref, h_ref, wg_ref, wu_ref, wd_ref):
    for c in range(FFN_NC):
        cols = slice(c * FFN_FC, (c + 1) * FFN_FC)
        h = h_ref[...]
        g = jnp.dot(h, wg_ref[:, cols], preferred_element_type=F32)
        u = jnp.dot(h, wu_ref[:, cols], preferred_element_type=F32)
        a = (_silu(g) * u).astype(BF16)
        y = jnp.dot(a, wd_ref[cols, :], preferred_element_type=F32)
        if c == 0:
            acc_ref[...] = y
        else:
            acc_ref[...] += y


def _ffn1_kernel(x_ref, sh_ref, sc_ref, gt_ref, nw_ref, wg_ref, wu_ref, wd_ref, o_ref, h_ref, acc_ref):
    x = x_ref[...]
    h_ref[...] = (x * _rms_rows(x) * (nw_ref[...] * (1.0 + sc_ref[...])) + sh_ref[...]).astype(BF16)
    _swiglu_into(acc_ref, h_ref, wg_ref, wu_ref, wd_ref)
    o_ref[...] = x_ref[...] + (0.5 * gt_ref[...]) * acc_ref[...]


def _ffn2_kernel(x_ref, oa_ref, ob_ref, woa_ref, wob_ref, g2_ref, sh_ref, sc_ref, gt_ref, nw_ref,
                 wg_ref, wu_ref, wd_ref, fw_ref, o_ref, h_ref, acc_ref, x2_ref):
    for qb in range(ob_ref.shape[1]):
        rows = slice(qb * MLA_TQ, (qb + 1) * MLA_TQ)
        y = lax.dot_general(oa_ref[:, rows], woa_ref[...], _TN, preferred_element_type=F32)
        ob = ob_ref[:, qb].reshape(N_HEADS_B * V_DIM_B, MLA_TQ)
        y = y + lax.dot_general(ob, wob_ref[...], _TN, preferred_element_type=F32)
        x2_ref[rows, :] = x_ref[rows, :] + g2_ref[...] * y
    x = x2_ref[...]
    h_ref[...] = (x * _rms_rows(x) * (nw_ref[...] * (1.0 + sc_ref[...])) + sh_ref[...]).astype(BF16)
    _swiglu_into(acc_ref, h_ref, wg_ref, wu_ref, wd_ref)
    z = x2_ref[...] + (0.5 * gt_ref[...]) * acc_ref[...]
    o_ref[...] = z * _rms_rows(z) * fw_ref[...]


def _row_spec():
    return pl.BlockSpec((None, 1, D_MODEL), lambda b, t: (b, 0, 0))


def _ffn_weight_specs():
    return [_const_spec((D_MODEL, D_FF)), _const_spec((D_MODEL, D_FF)), _const_spec((D_FF, D_MODEL))]


def _ffn1(x, sh, sc, gt, nw, wg, wu, wd):
    B, S, _ = x.shape
    tm = FFN_TM
    tok = pl.BlockSpec((None, tm, D_MODEL), lambda b, t: (b, t, 0))
    return pl.pallas_call(
        _ffn1_kernel,
        grid=(B, S // tm),
        in_specs=[tok, _row_spec(), _row_spec(), _row_spec(), _const_spec((1, D_MODEL))] + _ffn_weight_specs(),
        out_specs=tok,
        out_shape=jax.ShapeDtypeStruct(x.shape, F32),
        scratch_shapes=[pltpu.VMEM((tm, D_MODEL), BF16), pltpu.VMEM((tm, D_MODEL), F32)],
        compiler_params=_cparams(("arbitrary", "arbitrary")),
        name="ffn1",
    )(x, sh, sc, gt, nw, wg, wu, wd)


def _ffn2(x, oa_t, ob_t, woa, wob, g2, sh, sc, gt, nw, wg, wu, wd, fw):
    B, S, _ = x.shape
    tm = FFN_TM
    assert tm % MLA_TQ == 0
    tok = pl.BlockSpec((None, tm, D_MODEL), lambda b, t: (b, t, 0))
    att_a = pl.BlockSpec((None, A_Q, tm), lambda b, t: (b, 0, t))
    att_b = pl.BlockSpec((None, N_HEADS_B, tm // MLA_TQ, V_DIM_B, MLA_TQ), lambda b, t: (b, 0, t, 0, 0))
    return pl.pallas_call(
        _ffn2_kernel,
        grid=(B, S // tm),
        in_specs=[tok, att_a, att_b, _const_spec((A_Q, D_MODEL)), _const_spec((A_Q, D_MODEL)),
                  _row_spec(), _row_spec(), _row_spec(), _row_spec(), _const_spec((1, D_MODEL))]
                 + _ffn_weight_specs() + [_const_spec((1, D_MODEL))],
        out_specs=tok,
        out_shape=jax.ShapeDtypeStruct(x.shape, F32),
        scratch_shapes=[pltpu.VMEM((tm, D_MODEL), BF16), pltpu.VMEM((tm, D_MODEL), F32),
                        pltpu.VMEM((tm, D_MODEL), F32)],
        compiler_params=_cparams(("arbitrary", "arbitrary")),
        name="ffn2",
    )(x, oa_t, ob_t, woa, wob, g2, sh, sc, gt, nw, wg, wu, wd, fw)


def _rms_cols(x, n):
    return lax.rsqrt(jnp.sum(x * x, axis=0, keepdims=True) * (1.0 / n) + EPS)


def _rope_cols(y, cos, sin):
    half = y.shape[0] // 2
    y1, y2 = y[:half], y[half:]
    return jnp.concatenate([y1 * cos - y2 * sin, y2 * cos + y1 * sin], axis=0)


def _inproj_kernel(x_ref, sh_ref, sc_ref, nw_ref, w1_ref, wuq_ref, wukv_ref, g_ref, t_ref,
                   qa_ref, ka_ref, va_ref, qb_ref, kb_ref, vb_ref,
                   zt_ref, cq_ref, ckv_ref, qbt_ref, kvt_ref):
    n = PROJ_SUB

    def tables(tok):
        return (t_ref[T_CA:T_CA + 32, tok], t_ref[T_SA:T_SA + 32, tok],
                t_ref[T_CB:T_CB + 16, tok], t_ref[T_SB:T_SB + 16, tok])

    def project(sub):
        x = x_ref[sub * n:(sub + 1) * n, :]
        h = (x * _rms_rows(x) * (nw_ref[...] * (1.0 + sc_ref[...])) + sh_ref[...]).astype(BF16)
        zt_ref[sub] = lax.dot_general(w1_ref[...], h, _NT, preferred_element_type=F32)

    def swa_heads(sub):
        off = sub * n
        tok = slice(off, off + n)
        cos_a, sin_a, _, _ = tables(tok)
        zt = zt_ref.at[sub]
        g_qa = g_ref[G_QA:G_QA + HEAD_DIM_A, :]
        for hd in range(N_HEADS_A):
            blk = zt[R_QA + hd * HEAD_DIM_A:R_QA + (hd + 1) * HEAD_DIM_A, :]
            r = _rms_cols(blk, HEAD_DIM_A) * SWA_Q_SCALE
            qa_ref[hd, :, tok] = (_rope_cols(blk * g_qa, cos_a, sin_a) * r).astype(BF16)
        g_ka = g_ref[G_KA:G_KA + HEAD_DIM_A, :]
        ks = []
        for hd in range(N_KV_A):
            blk = zt[R_KA + hd * HEAD_DIM_A:R_KA + (hd + 1) * HEAD_DIM_A, :]
            ks.append(_rope_cols(blk * g_ka, cos_a, sin_a) * _rms_cols(blk, HEAD_DIM_A))
        ka_ref[tok, :] = jnp.concatenate(ks, axis=0).T.astype(BF16)
        vt = zt[R_VA:R_VA + A_KV, :].astype(BF16)
        for c in range(n // LANE):
            va_ref[off // LANE + c] = vt[:, c * LANE:(c + 1) * LANE]
        cq = zt[R_CQ:R_CQ + Q_LORA, :]
        cq_ref[sub] = (cq * g_ref[G_QL:G_QL + Q_LORA, :] * _rms_cols(cq, Q_LORA)).astype(BF16)
        ckv = zt[R_CKV:R_CKV + KV_LORA, :]
        ckv_ref[sub] = (ckv * g_ref[G_KVL:G_KVL + KV_LORA, :] * _rms_cols(ckv, KV_LORA)).astype(BF16)

    def up_project(sub):
        qbt_ref[sub] = jnp.dot(wuq_ref[...], cq_ref[sub], preferred_element_type=F32)
        kvt_ref[sub] = jnp.dot(wukv_ref[...], ckv_ref[sub], preferred_element_type=F32)

    def mla_heads(sub):
        off = sub * n
        tok = slice(off, off + n)
        _, _, cos_b, sin_b = tables(tok)
        g_mq = g_ref[G_MQ:G_MQ + QK_DIM_B, :]
        pad = jnp.zeros((QK_PAD_B - QK_DIM_B, n), BF16)
        for hd in range(N_HEADS_B):
            blk = qbt_ref[sub, hd * QK_PAD_B:hd * QK_PAD_B + QK_DIM_B, :]
            y = blk * g_mq
            o = jnp.concatenate([y[:NOPE_B], _rope_cols(y[NOPE_B:], cos_b, sin_b)], axis=0)
            o = (o * (_rms_cols(blk, QK_DIM_B) * MLA_Q_SCALE)).astype(BF16)
            step = min(n, MLA_TQ)
            for c in range(n // step):
                lo = off + c * step
                qb_ref[hd, lo // MLA_TQ, :, lo % MLA_TQ:lo % MLA_TQ + step] = (
                    jnp.concatenate([o[:, c * step:(c + 1) * step], pad[:, :step]], axis=0))
        kr = zt_ref[sub, R_KR:R_KR + ROPE_B, :]
        ss_kr = jnp.sum(kr * kr, axis=0, keepdims=True)
        kr_rot = _rope_cols(kr * g_ref[G_KR:G_KR + ROPE_B, :], cos_b, sin_b)
        g_kn = g_ref[G_KN:G_KN + NOPE_B, :]
        padf = jnp.zeros((QK_PAD_B - QK_DIM_B, n), F32)
        for hd in range(N_HEADS_B):
            kn = kvt_ref[sub, hd * 2 * NOPE_B:hd * 2 * NOPE_B + NOPE_B, :]
            r = lax.rsqrt((jnp.sum(kn * kn, axis=0, keepdims=True) + ss_kr) * (1.0 / QK_DIM_B) + EPS)
            kt = jnp.concatenate([kn * g_kn * r, kr_rot * r, padf], axis=0)
            kb_ref[hd, tok, :] = kt.T.astype(BF16)
            v = kvt_ref[sub, hd * 2 * NOPE_B + NOPE_B:(hd + 1) * 2 * NOPE_B, :].astype(BF16)
            step = min(n, MLA_TK)
            for c in range(n // step):
                lo = off + c * step
                vb_ref[hd, lo // MLA_TK, :, lo % MLA_TK:lo % MLA_TK + step] = v[:, c * step:(c + 1) * step]

    assert PROJ_TM == 2 * PROJ_SUB
    project(0)
    project(1)
    swa_heads(0)
    up_project(0)
    swa_heads(1)
    mla_heads(0)
    up_project(1)
    mla_heads(1)


def _inproj(x, sh, sc, nw, w1t, wuqt, wukvt, gains, table):
    B, S, _ = x.shape
    tm = PROJ_TM
    nt = S // tm
    out_shape = [
        jax.ShapeDtypeStruct((B, N_HEADS_A, HEAD_DIM_A, S), BF16),
        jax.ShapeDtypeStruct((B, S, A_KV), BF16),
        jax.ShapeDtypeStruct((B, S // LANE, A_KV, LANE), BF16),
        jax.ShapeDtypeStruct((B, N_HEADS_B, S // MLA_TQ, QK_PAD_B, MLA_TQ), BF16),
        jax.ShapeDtypeStruct((B, N_HEADS_B, S, QK_PAD_B), BF16),
        jax.ShapeDtypeStruct((B, N_HEADS_B, S // MLA_TK, V_DIM_B, MLA_TK), BF16),
    ]
    out_specs = [
        pl.BlockSpec((None, N_HEADS_A, HEAD_DIM_A, tm), lambda b, t: (b, 0, 0, t)),
        pl.BlockSpec((None, tm, A_KV), lambda b, t: (b, t, 0)),
        pl.BlockSpec((None, tm // LANE, A_KV, LANE), lambda b, t: (b, t, 0, 0)),
        pl.BlockSpec((None, N_HEADS_B, tm // MLA_TQ, QK_PAD_B, MLA_TQ), lambda b, t: (b, 0, t, 0, 0)),
        pl.BlockSpec((None, N_HEADS_B, tm, QK_PAD_B), lambda b, t: (b, 0, t, 0)),
        pl.BlockSpec((None, N_HEADS_B, tm // MLA_TK, V_DIM_B, MLA_TK), lambda b, t: (b, 0, t, 0, 0)),
    ]
    return pl.pallas_call(
        _inproj_kernel,
        grid=(B, nt),
        in_specs=[
            pl.BlockSpec((None, tm, D_MODEL), lambda b, t: (b, t, 0)),
            _row_spec(), _row_spec(), _const_spec((1, D_MODEL)),
            _const_spec((IN_WIDTH, D_MODEL)),
            _const_spec((N_HEADS_B * QK_PAD_B, Q_LORA)),
            _const_spec((N_HEADS_B * 2 * NOPE_B, KV_LORA)),
            _const_spec((G_ROWS, PROJ_SUB)),
            pl.BlockSpec((T_ROWS, tm), lambda b, t: (0, t)),
        ],
        out_specs=out_specs,
        out_shape=out_shape,
        scratch_shapes=[
            pltpu.VMEM((2, IN_WIDTH, PROJ_SUB), F32),
            pltpu.VMEM((2, Q_LORA, PROJ_SUB), BF16),
            pltpu.VMEM((2, KV_LORA, PROJ_SUB), BF16),
            pltpu.VMEM((2, N_HEADS_B * QK_PAD_B, PROJ_SUB), F32),
            pltpu.VMEM((2, N_HEADS_B * 2 * NOPE_B, PROJ_SUB), F32),
        ],
        compiler_params=_cparams(("arbitrary", "arbitrary")),
        name="inproj",
    )(x, sh, sc, nw, w1t, wuqt, wukvt, gains, table)


def _swa_kernel(*refs, n_blocks, bounded):
    if bounded:
        shift_ref, q_ref, k_ref, v_ref, sink_ref, o_ref, p_ref = refs
    else:
        q_ref, k_ref, v_ref, sink_ref, o_ref, p_ref, m_ref = refs
    tq = q_ref.shape[-1]
    nq = tq // LANE
    wide = GROUP_A * LANE
    row = lax.broadcasted_iota(jnp.int32, (LANE, wide), 0)
    col = lax.broadcasted_iota(jnp.int32, (LANE, wide), 1) & (LANE - 1)
    diff = row - col
    zeros = jnp.zeros((HEAD_DIM_A, LANE), BF16)
    ones = jnp.ones((ONES_ROWS, LANE), BF16)
    chains = [(i, j) for i in range(nq) for j in range(N_KV_A)]

    def blocks(i):
        nb = pl.program_id(1) * nq + i
        kb_idx = [jnp.maximum(nb - 1, 0), nb, jnp.minimum(nb + 1, n_blocks - 1)]
        lo = jnp.where(nb > 0, 0, 2 * LANE)
        hi = jnp.where(nb < n_blocks - 1, 0, -2 * LANE)
        return kb_idx, [diff >= lo, None, diff <= hi]

    def probs(i, j, slot):
        kb_idx, masks = blocks(i)
        cols = []
        for g in range(GROUP_A):
            qh = q_ref[j * GROUP_A + g, :, i * LANE:(i + 1) * LANE]
            cols.append(jnp.concatenate([qh, zeros] if j == 0 else [zeros, qh], axis=0))
        qp = jnp.concatenate(cols, axis=1)
        ss = []
        for c in range(3):
            k = k_ref[pl.ds(pl.multiple_of(kb_idx[c] * LANE, LANE), LANE), :]
            ss.append(jnp.dot(k, qp, preferred_element_type=F32))
        if bounded:
            m = shift_ref[0, 0]
        else:
            m = sink_ref[j]
            for c in range(3):
                sc = ss[c] if masks[c] is None else jnp.where(masks[c], ss[c], NEG_BIG)
                m = jnp.maximum(m, jnp.max(sc, axis=0, keepdims=True))
            m_ref[slot] = m
        for c in range(3):
            p = jnp.exp2(ss[c] - m)
            if masks[c] is not None:
                p = jnp.where(masks[c], p, 0.0)
            p_ref[slot, c] = p.astype(BF16)

    def apply(i, j, slot):
        kb_idx, _ = blocks(i)
        m = shift_ref[0, 0] if bounded else m_ref[slot]
        acc = None
        for c in range(3):
            vt = v_ref[kb_idx[c], j * HEAD_DIM_A:(j + 1) * HEAD_DIM_A, :]
            pv = jnp.dot(jnp.concatenate([vt, ones], axis=0), p_ref[slot, c], preferred_element_type=F32)
            acc = pv if acc is None else acc + pv
        l = acc[HEAD_DIM_A:HEAD_DIM_A + 1] + jnp.exp2(sink_ref[j] - m)
        o = acc[:HEAD_DIM_A] * (1.0 / l)
        for g in range(GROUP_A):
            o_ref[j * GROUP_A + g, :, i * LANE:(i + 1) * LANE] = o[:, g * LANE:(g + 1) * LANE].astype(BF16)

    probs(*chains[0], 0)
    for c, (i, j) in enumerate(chains):
        if c + 1 < len(chains):
            probs(*chains[c + 1], (c + 1) % 2)
        apply(i, j, c % 2)


def _swa_call(bounded, qa_t, ka, va_t, sink_rows, *scalars):
    B, _, _, S = qa_t.shape
    tq = SWA_TQ
    nbk = S // LANE
    qspec = pl.BlockSpec((None, N_HEADS_A, HEAD_DIM_A, tq), lambda b, t: (b, 0, 0, t))
    return pl.pallas_call(
        functools.partial(_swa_kernel, n_blocks=nbk, bounded=bounded),
        grid=(B, S // tq),
        in_specs=[pl.BlockSpec(memory_space=pltpu.SMEM)] * len(scalars) + [
            qspec,
            pl.BlockSpec((None, S, A_KV), lambda b, t: (b, 0, 0)),
            pl.BlockSpec((None, nbk, A_KV, LANE), lambda b, t: (b, 0, 0, 0)),
            _const_spec((N_KV_A, 1, GROUP_A * LANE)),
        ],
        out_specs=qspec,
        out_shape=jax.ShapeDtypeStruct(qa_t.shape, BF16),
        scratch_shapes=[pltpu.VMEM((2, 3, LANE, GROUP_A * LANE), BF16)]
                       + ([] if bounded else [pltpu.VMEM((2, 1, GROUP_A * LANE), F32)]),
        compiler_params=_cparams(("arbitrary", "arbitrary")),
        name="swa_bounded" if bounded else "swa_online",
    )(*scalars, qa_t, ka, va_t, sink_rows)


def _swa(qa_t, ka, va_t, sink_rows, score_bound):
    shift = jnp.maximum(score_bound, jnp.max(sink_rows))
    return lax.cond(
        shift + score_bound <= SOFTMAX_MAX_SHIFT_RANGE,
        lambda: _swa_call(True, qa_t, ka, va_t, sink_rows, shift.reshape(1, 1).astype(F32)),
        lambda: _swa_call(False, qa_t, ka, va_t, sink_rows),
    )


def _mla_online_kernel(q_ref, k_ref, v_ref, o_ref, m_ref, acc_ref):
    nq = q_ref.shape[0]
    n = v_ref.shape[0]
    tk = v_ref.shape[-1]
    ones = jnp.ones((ONES_ROWS, tk), BF16)

    def qblock(qb, carry):
        m_ref[...] = jnp.full_like(m_ref, NEG_BIG)
        acc_ref[...] = jnp.zeros_like(acc_ref)

        def chunk(c, carry2):
            k = k_ref[pl.ds(pl.multiple_of(c * tk, tk), tk), :]
            s = jnp.dot(k, q_ref[qb], preferred_element_type=F32)
            m_old = m_ref[...]
            m_new = jnp.maximum(m_old, jnp.max(s, axis=0, keepdims=True))
            p = jnp.exp2(s - m_new).astype(BF16)
            v1 = jnp.concatenate([v_ref[c], ones], axis=0)
            acc_ref[...] = jnp.exp2(m_old - m_new) * acc_ref[...] + jnp.dot(v1, p, preferred_element_type=F32)
            m_ref[...] = m_new
            return carry2

        lax.fori_loop(0, n, chunk, 0)
        acc = acc_ref[...]
        o_ref[qb] = (acc[:V_DIM_B] * (1.0 / acc[V_DIM_B:V_DIM_B + 1])).astype(BF16)
        return carry

    lax.fori_loop(0, nq, qblock, 0)


def _mla_bounded_kernel(shift_ref, q_ref, k_ref, v_ref, o_ref, p_ref):
    nq = q_ref.shape[0]
    n = v_ref.shape[0]
    tk = v_ref.shape[-1]
    total = nq * n
    log2n = n.bit_length() - 1
    qpb = min(nq, max(1, MLA_MAX_UNROLL // n))
    shift = shift_ref[0, 0]
    ones = jnp.ones((PV_LHS_ROWS - V_DIM_B, tk), BF16)

    def probs(t, slot):
        c = t & (n - 1)
        k = k_ref[pl.ds(pl.multiple_of(c * tk, tk), tk), :]
        s = jnp.dot(k, q_ref[t >> log2n], preferred_element_type=F32)
        p_ref[slot] = jnp.exp2(s - shift).astype(BF16)

    def pv(c, slot):
        v1 = jnp.concatenate([v_ref[c], ones], axis=0)
        return jnp.dot(v1, p_ref[slot], preferred_element_type=F32)

    probs(0, 0)

    def body(j, carry):
        t0 = j * (qpb * n)
        for u in range(qpb * n):
            probs(jnp.minimum(t0 + u + 1, total - 1), (u + 1) % 2)
            part = pv(u % n, u % 2)
            acc = part if u % n == 0 else acc + part
            if u % n == n - 1:
                o_ref[j * qpb + u // n] = (acc[:V_DIM_B] * (1.0 / acc[V_DIM_B:V_DIM_B + 1])).astype(BF16)
        return carry

    lax.fori_loop(0, nq // qpb, body, 0)


def _mla_call(kernel_fn, scratch, qb_t, kb, vb_t, *scalars):
    B, H, nq, _, tq = qb_t.shape
    nck, tk = vb_t.shape[2], vb_t.shape[4]
    S = nck * tk
    assert nck & (nck - 1) == 0 and nq & (nq - 1) == 0 and nck >= 2
    return pl.pallas_call(
        kernel_fn,
        grid=(B, H),
        in_specs=[pl.BlockSpec(memory_space=pltpu.SMEM)] * len(scalars) + [
            pl.BlockSpec((None, None, nq, QK_PAD_B, tq), lambda b, h: (b, h, 0, 0, 0)),
            pl.BlockSpec((None, None, S, QK_PAD_B), lambda b, h: (b, h, 0, 0)),
            pl.BlockSpec((None, None, nck, V_DIM_B, tk), lambda b, h: (b, h, 0, 0, 0)),
        ],
        out_specs=pl.BlockSpec((None, None, nq, V_DIM_B, tq), lambda b, h: (b, h, 0, 0, 0)),
        out_shape=jax.ShapeDtypeStruct((B, H, nq, V_DIM_B, tq), BF16),
        scratch_shapes=scratch(tk, tq),
        compiler_params=_cparams(("arbitrary", "arbitrary")),
        name=kernel_fn.__name__.strip("_").replace("_kernel", ""),
    )(*scalars, qb_t, kb, vb_t)


def _mla_online_scratch(tk, tq):
    return [pltpu.VMEM((1, tq), F32), pltpu.VMEM((V_DIM_B + ONES_ROWS, tq), F32)]


def _mla_bounded_scratch(tk, tq):
    return [pltpu.VMEM((2, tk, tq), BF16)]


def _mla(qb_t, kb, vb_t, score_bound):
    shift = score_bound.reshape(1, 1).astype(F32)
    return lax.cond(
        2.0 * score_bound <= SOFTMAX_MAX_SHIFT_RANGE,
        lambda: _mla_call(_mla_bounded_kernel, _mla_bounded_scratch, qb_t, kb, vb_t, shift),
        lambda: _mla_call(_mla_online_kernel, _mla_online_scratch, qb_t, kb, vb_t),
    )


def _rope_table(S):
    pos = jnp.arange(S, dtype=F32)

    def cs(d):
        inv = ROPE_THETA ** (-jnp.arange(0, d, 2, dtype=F32) / d)
        ang = pos[:, None] * inv[None, :]
        return jnp.cos(ang).T, jnp.sin(ang).T

    ca, sa = cs(HEAD_DIM_A)
    cb, sb = cs(ROPE_B)
    return jnp.concatenate([ca, sa, cb, sb], axis=0)


def _gain_tile(swa_q_norm, swa_k_norm, mla_q_lora_norm, mla_kv_lora_norm, mla_q_norm, mla_k_norm):
    rows = jnp.concatenate([
        swa_q_norm, swa_k_norm, mla_q_lora_norm, mla_kv_lora_norm,
        mla_q_norm, mla_k_norm[:NOPE_B], mla_k_norm[NOPE_B:],
    ]).astype(F32)
    return jnp.broadcast_to(rows[:, None], (G_ROWS, PROJ_SUB))


def _score_bound(dim, q_scale, q_gain, k_gain):
    return (BF16_ROUND_MARGIN * dim * q_scale * jnp.max(jnp.abs(q_gain)) * jnp.max(jnp.abs(k_gain))).astype(F32)


def kernel(x_prompt, x_sample, c_prompt, c_sample, ada_w, ada_b, ffn1_norm, ffn1_wg, ffn1_wu, ffn1_wd, mix_norm, w_in, swa_q_norm, swa_k_norm, swa_sink, mla_q_lora_norm, mla_w_uq, mla_kv_lora_norm, mla_w_ukv, mla_q_norm, mla_k_norm, w_out, ffn2_norm, ffn2_wg, ffn2_wu, ffn2_wd, final_norm):
    assert ada_w.shape[0] == 1, "single-layer trunk"
    bp, bs = x_prompt.shape[0], x_sample.shape[0]
    assert bp + bs <= MOD_ROWS

    f1 = (ffn1_wg[0].astype(BF16), ffn1_wu[0].astype(BF16), ffn1_wd[0].astype(BF16))
    f2 = (ffn2_wg[0].astype(BF16), ffn2_wu[0].astype(BF16), ffn2_wd[0].astype(BF16))
    w1t = w_in[0].T.astype(BF16)
    wuq = mla_w_uq[0].reshape(Q_LORA, N_HEADS_B, QK_DIM_B)
    wuq = jnp.pad(wuq, ((0, 0), (0, 0), (0, QK_PAD_B - QK_DIM_B))).reshape(Q_LORA, N_HEADS_B * QK_PAD_B)
    wuqt = wuq.T.astype(BF16)
    wukvt = mla_w_ukv[0].T.astype(BF16)
    woa = w_out[0, :A_Q].astype(BF16)
    wob = w_out[0, A_Q:].astype(BF16)
    gains = _gain_tile(swa_q_norm[0], swa_k_norm[0], mla_q_lora_norm[0], mla_kv_lora_norm[0],
                       mla_q_norm[0], mla_k_norm[0])
    sink_rows = jnp.repeat(swa_sink[0].astype(F32) * LOG2E, LANE).reshape(N_KV_A, 1, GROUP_A * LANE)
    row = lambda v: v.reshape(1, D_MODEL)
    bound_a = _score_bound(HEAD_DIM_A, SWA_Q_SCALE, swa_q_norm[0], swa_k_norm[0])
    bound_b = _score_bound(QK_DIM_B, MLA_Q_SCALE, mla_q_norm[0], mla_k_norm[0])

    c_all = jnp.concatenate([c_prompt, c_sample, jnp.zeros((MOD_ROWS - bp - bs, D_MODEL), F32)], axis=0)
    mod = _modulation(c_all, ada_w[0], ada_b[0].reshape(1, -1)).reshape(MOD_ROWS, N_MOD, 1, D_MODEL)

    def trunk(x, m):
        sh1, sc1, g1, sh2, sc2, g2, sh3, sc3, g3 = (m[:, i] for i in range(N_MOD))
        S = x.shape[1]
        x1 = _ffn1(x, sh1, sc1, g1, row(ffn1_norm[0]), *f1)
        qa_t, ka, va_t, qb_t, kb, vb_t = _inproj(x1, sh2, sc2, row(mix_norm[0]), w1t, wuqt, wukvt,
                                                  gains, _rope_table(S))
        oa_t = _swa(qa_t, ka, va_t, sink_rows, bound_a)
        ob_t = _mla(qb_t, kb, vb_t, bound_b)
        return _ffn2(x1, oa_t.reshape(-1, A_Q, S), ob_t, woa, wob, g2, sh3, sc3, g3,
                     row(ffn2_norm[0]), *f2, row(final_norm[0]))

    y_prompt = trunk(x_prompt, mod[:bp])
    y_sample = trunk(x_sample, mod[bp:bp + bs])
    return (y_prompt, y_sample)
```

```python
import functools
import math

import jax
import jax.numpy as jnp
from jax import lax
from jax.experimental import pallas as pl
from jax.experimental.pallas import tpu as pltpu

F32 = jnp.float32
BF16 = jnp.bfloat16

D_MODEL = 1024
D_FF = 2816
N_HEADS_A = 8
N_KV_A = 2
HEAD_DIM_A = 64
GROUP_A = N_HEADS_A // N_KV_A
WINDOW = 128
N_HEADS_B = 8
Q_LORA = 256
KV_LORA = 128
NOPE_B = 64
ROPE_B = 32
QK_DIM_B = NOPE_B + ROPE_B
V_DIM_B = 64
ROPE_THETA = 10000.0
EPS = 1e-6
N_MOD = 9
A_Q = N_HEADS_A * HEAD_DIM_A
A_KV = N_KV_A * HEAD_DIM_A
IN_WIDTH = A_Q + 2 * A_KV + Q_LORA + KV_LORA + ROPE_B
NEG_BIG = -1e30
LOG2E = math.log2(math.e)
SWA_Q_SCALE = LOG2E / math.sqrt(HEAD_DIM_A)
MLA_Q_SCALE = LOG2E / math.sqrt(QK_DIM_B)
SOFTMAX_MAX_SHIFT_RANGE = 100.0
BF16_ROUND_MARGIN = 1.02

LANE = 128
QK_PAD_B = LANE
VMEM_LIMIT_BYTES = 56 * 1024 * 1024

MOD_ROWS = 16
MOD_BN = 1152
FFN_TM = 1024
FFN_FC = 256
FFN_NC = D_FF // FFN_FC
PROJ_TM = 1024
PROJ_SUB = 512
SWA_TQ = 1024
MLA_TQ = 512
MLA_MAX_UNROLL = 16
MLA_TK = 2048
ONES_ROWS = 16
PV_LHS_ROWS = 128

R_QA, R_KA, R_VA = 0, A_Q, A_Q + A_KV
R_CQ = A_Q + 2 * A_KV
R_CKV = R_CQ + Q_LORA
R_KR = R_CKV + KV_LORA
G_QA, G_KA = 0, HEAD_DIM_A
G_QL = 2 * HEAD_DIM_A
G_KVL = G_QL + Q_LORA
G_MQ = G_KVL + KV_LORA
G_KN = G_MQ + QK_DIM_B
G_KR = G_KN + NOPE_B
G_ROWS = G_KR + ROPE_B
T_CA, T_SA = 0, HEAD_DIM_A // 2
T_CB = HEAD_DIM_A
T_SB = T_CB + ROPE_B // 2
T_ROWS = T_SB + ROPE_B // 2

_NT = (((1,), (1,)), ((), ()))
_TN = (((0,), (0,)), ((), ()))


def _cparams(sem):
    return pltpu.CompilerParams(dimension_semantics=sem, vmem_limit_bytes=VMEM_LIMIT_BYTES)


def _const_spec(shape):
    n = len(shape)
    return pl.BlockSpec(shape, lambda *_: (0,) * n, pipeline_mode=pl.Buffered(1))


def _silu(x):
    return x * (1.0 / (1.0 + jnp.exp(-x)))


def _rms_rows(x):
    return lax.rsqrt(jnp.mean(x * x, axis=-1, keepdims=True) + EPS)


def _mod_kernel(c_ref, w_ref, b_ref, o_ref):
    a = _silu(c_ref[...]).astype(BF16)
    o_ref[...] = jnp.dot(a, w_ref[...].astype(BF16), preferred_element_type=F32) + b_ref[...]


def _modulation(c, ada_w, ada_b):
    n = ada_w.shape[1]
    return pl.pallas_call(
        _mod_kernel,
        grid=(n // MOD_BN,),
        in_specs=[
            pl.BlockSpec((MOD_ROWS, D_MODEL), lambda j: (0, 0)),
            pl.BlockSpec((D_MODEL, MOD_BN), lambda j: (0, j)),
            pl.BlockSpec((1, MOD_BN), lambda j: (0, j)),
        ],
        out_specs=pl.BlockSpec((MOD_ROWS, MOD_BN), lambda j: (0, j)),
        out_shape=jax.ShapeDtypeStruct((MOD_ROWS, n), F32),
        compiler_params=_cparams(("arbitrary",)),
        name="mod",
    )(c, ada_w, ada_b)


def _swiglu_into(acc_ref, h_ref, wg_ref, wu_ref, wd_ref):
    for c in range(FFN_NC):
        cols = slice(c * FFN_FC, (c + 1) * FFN_FC)
        h = h_ref[...]
        g = jnp.dot(h, wg_ref[:, cols], preferred_element_type=F32)
        u = jnp.dot(h, wu_ref[:, cols], preferred_element_type=F32)
        a = (_silu(g) * u).astype(BF16)
        y = jnp.dot(a, wd_ref[cols, :], preferred_element_type=F32)
        if c == 0:
            acc_ref[...] = y
        else:
            acc_ref[...] += y


def _ffn1_kernel(x_ref, sh_ref, sc_ref, gt_ref, nw_ref, wg_ref, wu_ref, wd_ref, o_ref, h_ref, acc_ref):
    x = x_ref[...]
    h_ref[...] = (x * _rms_rows(x) * (nw_ref[...] * (1.0 + sc_ref[...])) + sh_ref[...]).astype(BF16)
    _swiglu_into(acc_ref, h_ref, wg_ref, wu_ref, wd_ref)
    o_ref[...] = x_ref[...] + (0.5 * gt_ref[...]) * acc_ref[...]


def _ffn2_kernel(x_ref, oa_ref, ob_ref, woa_ref, wob_ref, g2_ref, sh_ref, sc_ref, gt_ref, nw_ref,
                 wg_ref, wu_ref, wd_ref, fw_ref, o_ref, h_ref, acc_ref, x2_ref):
    for qb in range(ob_ref.shape[1]):
        rows = slice(qb * MLA_TQ, (qb + 1) * MLA_TQ)
        y = lax.dot_general(oa_ref[:, rows], woa_ref[...], _TN, preferred_element_type=F32)
        ob = ob_ref[:, qb].reshape(N_HEADS_B * V_DIM_B, MLA_TQ)
        y = y + lax.dot_general(ob, wob_ref[...], _TN, preferred_element_type=F32)
        x2_ref[rows, :] = x_ref[rows, :] + g2_ref[...] * y
    x = x2_ref[...]
    h_ref[...] = (x * _rms_rows(x) * (nw_ref[...] * (1.0 + sc_ref[...])) + sh_ref[...]).astype(BF16)
    _swiglu_into(acc_ref, h_ref, wg_ref, wu_ref, wd_ref)
    z = x2_ref[...] + (0.5 * gt_ref[...]) * acc_ref[...]
    o_ref[...] = z * _rms_rows(z) * fw_ref[...]


def _row_spec():
    return pl.BlockSpec((None, 1, D_MODEL), lambda b, t: (b, 0, 0))


def _ffn_weight_specs():
    return [_const_spec((D_MODEL, D_FF)), _const_spec((D_MODEL, D_FF)), _const_spec((D_FF, D_MODEL))]


def _ffn1(x, sh, sc, gt, nw, wg, wu, wd):
    B, S, _ = x.shape
    tm = FFN_TM
    tok = pl.BlockSpec((None, tm, D_MODEL), lambda b, t: (b, t, 0))
    return pl.pallas_call(
        _ffn1_kernel,
        grid=(B, S // tm),
        in_specs=[tok, _row_spec(), _row_spec(), _row_spec(), _const_spec((1, D_MODEL))] + _ffn_weight_specs(),
        out_specs=tok,
        out_shape=jax.ShapeDtypeStruct(x.shape, F32),
        scratch_shapes=[pltpu.VMEM((tm, D_MODEL), BF16), pltpu.VMEM((tm, D_MODEL), F32)],
        compiler_params=_cparams(("arbitrary", "arbitrary")),
        name="ffn1",
    )(x, sh, sc, gt, nw, wg, wu, wd)


def _ffn2(x, oa_t, ob_t, woa, wob, g2, sh, sc, gt, nw, wg, wu, wd, fw):
    B, S, _ = x.shape
    tm = FFN_TM
    assert tm % MLA_TQ == 0
    tok = pl.BlockSpec((None, tm, D_MODEL), lambda b, t: (b, t, 0))
    att_a = pl.BlockSpec((None, A_Q, tm), lambda b, t: (b, 0, t))
    att_b = pl.BlockSpec((None, N_HEADS_B, tm // MLA_TQ, V_DIM_B, MLA_TQ), lambda b, t: (b, 0, t, 0, 0))
    return pl.pallas_call(
        _ffn2_kernel,
        grid=(B, S // tm),
        in_specs=[tok, att_a, att_b, _const_spec((A_Q, D_MODEL)), _const_spec((A_Q, D_MODEL)),
                  _row_spec(), _row_spec(), _row_spec(), _row_spec(), _const_spec((1, D_MODEL))]
                 + _ffn_weight_specs() + [_const_spec((1, D_MODEL))],
        out_specs=tok,
        out_shape=jax.ShapeDtypeStruct(x.shape, F32),
        scratch_shapes=[pltpu.VMEM((tm, D_MODEL), BF16), pltpu.VMEM((tm, D_MODEL), F32),
                        pltpu.VMEM((tm, D_MODEL), F32)],
        compiler_params=_cparams(("arbitrary", "arbitrary")),
        name="ffn2",
    )(x, oa_t, ob_t, woa, wob, g2, sh, sc, gt, nw, wg, wu, wd, fw)


def _rms_cols(x, n):
    return lax.rsqrt(jnp.sum(x * x, axis=0, keepdims=True) * (1.0 / n) + EPS)


def _rope_cols(y, cos, sin):
    half = y.shape[0] // 2
    y1, y2 = y[:half], y[half:]
    return jnp.concatenate([y1 * cos - y2 * sin, y2 * cos + y1 * sin], axis=0)


def _inproj_kernel(x_ref, sh_ref, sc_ref, nw_ref, w1_ref, wuq_ref, wukv_ref, g_ref, t_ref,
                   qa_ref, ka_ref, va_ref, qb_ref, kb_ref, vb_ref,
                   zt_ref, cq_ref, ckv_ref, qbt_ref, kvt_ref):
    n = PROJ_SUB

    def tables(tok):
        return (t_ref[T_CA:T_CA + 32, tok], t_ref[T_SA:T_SA + 32, tok],
                t_ref[T_CB:T_CB + 16, tok], t_ref[T_SB:T_SB + 16, tok])

    def project(sub):
        x = x_ref[sub * n:(sub + 1) * n, :]
        h = (x * _rms_rows(x) * (nw_ref[...] * (1.0 + sc_ref[...])) + sh_ref[...]).astype(BF16)
        zt_ref[sub] = lax.dot_general(w1_ref[...], h, _NT, preferred_element_type=F32)

    def swa_heads(sub):
        off = sub * n
        tok = slice(off, off + n)
        cos_a, sin_a, _, _ = tables(tok)
        zt = zt_ref.at[sub]
        g_qa = g_ref[G_QA:G_QA + HEAD_DIM_A, :]
        for hd in range(N_HEADS_A):
            blk = zt[R_QA + hd * HEAD_DIM_A:R_QA + (hd + 1) * HEAD_DIM_A, :]
            r = _rms_cols(blk, HEAD_DIM_A) * SWA_Q_SCALE
            qa_ref[hd, :, tok] = (_rope_cols(blk * g_qa, cos_a, sin_a) * r).astype(BF16)
        g_ka = g_ref[G_KA:G_KA + HEAD_DIM_A, :]
        ks = []
        for hd in range(N_KV_A):
            blk = zt[R_KA + hd * HEAD_DIM_A:R_KA + (hd + 1) * HEAD_DIM_A, :]
            ks.append(_rope_cols(blk * g_ka, cos_a, sin_a) * _rms_cols(blk, HEAD_DIM_A))
        ka_ref[tok, :] = jnp.concatenate(ks, axis=0).T.astype(BF16)
        vt = zt[R_VA:R_VA + A_KV, :].astype(BF16)
        for c in range(n // LANE):
            va_ref[off // LANE + c] = vt[:, c * LANE:(c + 1) * LANE]
        cq = zt[R_CQ:R_CQ + Q_LORA, :]
        cq_ref[sub] = (cq * g_ref[G_QL:G_QL + Q_LORA, :] * _rms_cols(cq, Q_LORA)).astype(BF16)
        ckv = zt[R_CKV:R_CKV + KV_LORA, :]
        ckv_ref[sub] = (ckv * g_ref[G_KVL:G_KVL + KV_LORA, :] * _rms_cols(ckv, KV_LORA)).astype(BF16)

    def up_project(sub):
        qbt_ref[sub] = jnp.dot(wuq_ref[...], cq_ref[sub], preferred_element_type=F32)
        kvt_ref[sub] = jnp.dot(wukv_ref[...], ckv_ref[sub], preferred_element_type=F32)

    def mla_heads(sub):
        off = sub * n
        tok = slice(off, off + n)
        _, _, cos_b, sin_b = tables(tok)
        g_mq = g_ref[G_MQ:G_MQ + QK_DIM_B, :]
        pad = jnp.zeros((QK_PAD_B - QK_DIM_B, n), BF16)
        for hd in range(N_HEADS_B):
            blk = qbt_ref[sub, hd * QK_PAD_B:hd * QK_PAD_B + QK_DIM_B, :]
            y = blk * g_mq
            o = jnp.concatenate([y[:NOPE_B], _rope_cols(y[NOPE_B:], cos_b, sin_b)], axis=0)
            o = (o * (_rms_cols(blk, QK_DIM_B) * MLA_Q_SCALE)).astype(BF16)
            step = min(n, MLA_TQ)
            for c in range(n // step):
                lo = off + c * step
                qb_ref[hd, lo // MLA_TQ, :, lo % MLA_TQ:lo % MLA_TQ + step] = (
                    jnp.concatenate([o[:, c * step:(c + 1) * step], pad[:, :step]], axis=0))
        kr = zt_ref[sub, R_KR:R_KR + ROPE_B, :]
        ss_kr = jnp.sum(kr * kr, axis=0, keepdims=True)
        kr_rot = _rope_cols(kr * g_ref[G_KR:G_KR + ROPE_B, :], cos_b, sin_b)
        g_kn = g_ref[G_KN:G_KN + NOPE_B, :]
        padf = jnp.zeros((QK_PAD_B - QK_DIM_B, n), F32)
        for hd in range(N_HEADS_B):
            kn = kvt_ref[sub, hd * 2 * NOPE_B:hd * 2 * NOPE_B + NOPE_B, :]
            r = lax.rsqrt((jnp.sum(kn * kn, axis=0, keepdims=True) + ss_kr) * (1.0 / QK_DIM_B) + EPS)
            kt = jnp.concatenate([kn * g_kn * r, kr_rot * r, padf], axis=0)
            kb_ref[hd, tok, :] = kt.T.astype(BF16)
            v = kvt_ref[sub, hd * 2 * NOPE_B + NOPE_B:(hd + 1) * 2 * NOPE_B, :].astype(BF16)
            vb_ref[hd, :, tok] = v

    assert PROJ_TM == 2 * PROJ_SUB
    project(0)
    project(1)
    swa_heads(0)
    up_project(0)
    swa_heads(1)
    mla_heads(0)
    up_project(1)
    mla_heads(1)


def _inproj(x, sh, sc, nw, w1t, wuqt, wukvt, gains, table):
    B, S, _ = x.shape
    tm = PROJ_TM
    nt = S // tm
    assert MLA_TK % tm == 0
    tiles_per_chunk = MLA_TK // tm
    out_shape = [
        jax.ShapeDtypeStruct((B, N_HEADS_A, HEAD_DIM_A, S), BF16),
        jax.ShapeDtypeStruct((B, S, A_KV), BF16),
        jax.ShapeDtypeStruct((B, S // LANE, A_KV, LANE), BF16),
        jax.ShapeDtypeStruct((B, N_HEADS_B, S // MLA_TQ, QK_PAD_B, MLA_TQ), BF16),
        jax.ShapeDtypeStruct((B, N_HEADS_B, S, QK_PAD_B), BF16),
        jax.ShapeDtypeStruct((B, N_HEADS_B, S // MLA_TK, V_DIM_B, MLA_TK), BF16),
    ]
    out_specs = [
        pl.BlockSpec((None, N_HEADS_A, HEAD_DIM_A, tm), lambda b, t: (b, 0, 0, t)),
        pl.BlockSpec((None, tm, A_KV), lambda b, t: (b, t, 0)),
        pl.BlockSpec((None, tm // LANE, A_KV, LANE), lambda b, t: (b, t, 0, 0)),
        pl.BlockSpec((None, N_HEADS_B, tm // MLA_TQ, QK_PAD_B, MLA_TQ), lambda b, t: (b, 0, t, 0, 0)),
        pl.BlockSpec((None, N_HEADS_B, tm, QK_PAD_B), lambda b, t: (b, 0, t, 0)),
        pl.BlockSpec((None, N_HEADS_B, None, V_DIM_B, tm), lambda b, t: (b, 0, t // tiles_per_chunk, 0, t % tiles_per_chunk)),
    ]
    return pl.pallas_call(
        _inproj_kernel,
        grid=(B, nt),
        in_specs=[
            pl.BlockSpec((None, tm, D_MODEL), lambda b, t: (b, t, 0)),
            _row_spec(), _row_spec(), _const_spec((1, D_MODEL)),
            _const_spec((IN_WIDTH, D_MODEL)),
            _const_spec((N_HEADS_B * QK_PAD_B, Q_LORA)),
            _const_spec((N_HEADS_B * 2 * NOPE_B, KV_LORA)),
            _const_spec((G_ROWS, PROJ_SUB)),
            pl.BlockSpec((T_ROWS, tm), lambda b, t: (0, t)),
        ],
        out_specs=out_specs,
        out_shape=out_shape,
        scratch_shapes=[
            pltpu.VMEM((2, IN_WIDTH, PROJ_SUB), F32),
            pltpu.VMEM((2, Q_LORA, PROJ_SUB), BF16),
            pltpu.VMEM((2, KV_LORA, PROJ_SUB), BF16),
            pltpu.VMEM((2, N_HEADS_B * QK_PAD_B, PROJ_SUB), F32),
            pltpu.VMEM((2, N_HEADS_B * 2 * NOPE_B, PROJ_SUB), F32),
        ],
        compiler_params=_cparams(("arbitrary", "arbitrary")),
        name="inproj",
    )(x, sh, sc, nw, w1t, wuqt, wukvt, gains, table)


def _swa_kernel(*refs, n_blocks, bounded):
    if bounded:
        shift_ref, q_ref, k_ref, v_ref, sink_ref, o_ref, p_ref = refs
    else:
        q_ref, k_ref, v_ref, sink_ref, o_ref, p_ref, m_ref = refs
    tq = q_ref.shape[-1]
    nq = tq // LANE
    wide = GROUP_A * LANE
    row = lax.broadcasted_iota(jnp.int32, (LANE, wide), 0)
    col = lax.broadcasted_iota(jnp.int32, (LANE, wide), 1) & (LANE - 1)
    diff = row - col
    zeros = jnp.zeros((HEAD_DIM_A, LANE), BF16)
    ones = jnp.ones((ONES_ROWS, LANE), BF16)
    chains = [(i, j) for i in range(nq) for j in range(N_KV_A)]

    def blocks(i):
        nb = pl.program_id(1) * nq + i
        kb_idx = [jnp.maximum(nb - 1, 0), nb, jnp.minimum(nb + 1, n_blocks - 1)]
        lo = jnp.where(nb > 0, 0, 2 * LANE)
        hi = jnp.where(nb < n_blocks - 1, 0, -2 * LANE)
        return kb_idx, [diff >= lo, None, diff <= hi]

    def probs(i, j, slot):
        kb_idx, masks = blocks(i)
        cols = []
        for g in range(GROUP_A):
            qh = q_ref[j * GROUP_A + g, :, i * LANE:(i + 1) * LANE]
            cols.append(jnp.concatenate([qh, zeros] if j == 0 else [zeros, qh], axis=0))
        qp = jnp.concatenate(cols, axis=1)
        ss = []
        for c in range(3):
            k = k_ref[pl.ds(pl.multiple_of(kb_idx[c] * LANE, LANE), LANE), :]
            ss.append(jnp.dot(k, qp, preferred_element_type=F32))
        if bounded:
            m = shift_ref[0, 0]
        else:
            m = sink_ref[j]
            for c in range(3):
                sc = ss[c] if masks[c] is None else jnp.where(masks[c], ss[c], NEG_BIG)
                m = jnp.maximum(m, jnp.max(sc, axis=0, keepdims=True))
            m_ref[slot] = m
        for c in range(3):
            p = jnp.exp2(ss[c] - m)
            if masks[c] is not None:
                p = jnp.where(masks[c], p, 0.0)
            p_ref[slot, c] = p.astype(BF16)

    def apply(i, j, slot):
        kb_idx, _ = blocks(i)
        m = shift_ref[0, 0] if bounded else m_ref[slot]
        acc = None
        for c in range(3):
            vt = v_ref[kb_idx[c], j * HEAD_DIM_A:(j + 1) * HEAD_DIM_A, :]
            pv = jnp.dot(jnp.concatenate([vt, ones], axis=0), p_ref[slot, c], preferred_element_type=F32)
            acc = pv if acc is None else acc + pv
        l = acc[HEAD_DIM_A:HEAD_DIM_A + 1] + jnp.exp2(sink_ref[j] - m)
        o = acc[:HEAD_DIM_A] * (1.0 / l)
        for g in range(GROUP_A):
            o_ref[j * GROUP_A + g, :, i * LANE:(i + 1) * LANE] = o[:, g * LANE:(g + 1) * LANE].astype(BF16)

    probs(*chains[0], 0)
    for c, (i, j) in enumerate(chains):
        if c + 1 < len(chains):
            probs(*chains[c + 1], (c + 1) % 2)
        apply(i, j, c % 2)


def _swa_call(bounded, qa_t, ka, va_t, sink_rows, *scalars):
    B, _, _, S = qa_t.shape
    tq = SWA_TQ
    nbk = S // LANE
    qspec = pl.BlockSpec((None, N_HEADS_A, HEAD_DIM_A, tq), lambda b, t: (b, 0, 0, t))
    return pl.pallas_call(
        functools.partial(_swa_kernel, n_blocks=nbk, bounded=bounded),
        grid=(B, S // tq),
        in_specs=[pl.BlockSpec(memory_space=pltpu.SMEM)] * len(scalars) + [
            qspec,
            pl.BlockSpec((None, S, A_KV), lambda b, t: (b, 0, 0)),
            pl.BlockSpec((None, nbk, A_KV, LANE), lambda b, t: (b, 0, 0, 0)),
            _const_spec((N_KV_A, 1, GROUP_A * LANE)),
        ],
        out_specs=qspec,
        out_shape=jax.ShapeDtypeStruct(qa_t.shape, BF16),
        scratch_shapes=[pltpu.VMEM((2, 3, LANE, GROUP_A * LANE), BF16)]
                       + ([] if bounded else [pltpu.VMEM((2, 1, GROUP_A * LANE), F32)]),
        compiler_params=_cparams(("arbitrary", "arbitrary")),
        name="swa_bounded" if bounded else "swa_online",
    )(*scalars, qa_t, ka, va_t, sink_rows)


def _swa(qa_t, ka, va_t, sink_rows, score_bound):
    shift = jnp.maximum(score_bound, jnp.max(sink_rows))
    return lax.cond(
        shift + score_bound <= SOFTMAX_MAX_SHIFT_RANGE,
        lambda: _swa_call(True, qa_t, ka, va_t, sink_rows, shift.reshape(1, 1).astype(F32)),
        lambda: _swa_call(False, qa_t, ka, va_t, sink_rows),
    )


def _mla_online_kernel(q_ref, k_ref, v_ref, o_ref, m_ref, acc_ref):
    nq = q_ref.shape[0]
    n = v_ref.shape[0]
    tk = v_ref.shape[-1]
    ones = jnp.ones((ONES_ROWS, tk), BF16)

    def qblock(qb, carry):
        m_ref[...] = jnp.full_like(m_ref, NEG_BIG)
        acc_ref[...] = jnp.zeros_like(acc_ref)

        def chunk(c, carry2):
            k = k_ref[pl.ds(pl.multiple_of(c * tk, tk), tk), :]
            s = jnp.dot(k, q_ref[qb], preferred_element_type=F32)
            m_old = m_ref[...]
            m_new = jnp.maximum(m_old, jnp.max(s, axis=0, keepdims=True))
            p = jnp.exp2(s - m_new).astype(BF16)
            v1 = jnp.concatenate([v_ref[c], ones], axis=0)
            acc_ref[...] = jnp.exp2(m_old - m_new) * acc_ref[...] + jnp.dot(v1, p, preferred_element_type=F32)
            m_ref[...] = m_new
            return carry2

        lax.fori_loop(0, n, chunk, 0)
        acc = acc_ref[...]
        o_ref[qb] = (acc[:V_DIM_B] * (1.0 / acc[V_DIM_B:V_DIM_B + 1])).astype(BF16)
        return carry

    lax.fori_loop(0, nq, qblock, 0)


def _mla_bounded_kernel(shift_ref, q_ref, k_ref, v_ref, o_ref, p_ref):
    nq = q_ref.shape[0]
    n = v_ref.shape[0]
    tk = v_ref.shape[-1]
    total = nq * n
    log2n = n.bit_length() - 1
    qpb = min(nq, max(1, MLA_MAX_UNROLL // n))
    shift = shift_ref[0, 0]
    ones = jnp.ones((PV_LHS_ROWS - V_DIM_B, tk), BF16)

    def probs(t, slot):
        c = t & (n - 1)
        k = k_ref[pl.ds(pl.multiple_of(c * tk, tk), tk), :]
        s = jnp.dot(k, q_ref[t >> log2n], preferred_element_type=F32)
        p_ref[slot] = jnp.exp2(s - shift).astype(BF16)

    def pv(c, slot):
        v1 = jnp.concatenate([v_ref[c], ones], axis=0)
        return jnp.dot(v1, p_ref[slot], preferred_element_type=F32)

    probs(0, 0)

    def body(j, carry):
        t0 = j * (qpb * n)
        for u in range(qpb * n):
            probs(jnp.minimum(t0 + u + 1, total - 1), (u + 1) % 2)
            part = pv(u % n, u % 2)
            acc = part if u % n == 0 else acc + part
            if u % n == n - 1:
                o_ref[j * qpb + u // n] = (acc[:V_DIM_B] * (1.0 / acc[V_DIM_B:V_DIM_B + 1])).astype(BF16)
        return carry

    lax.fori_loop(0, nq // qpb, body, 0)


def _mla_call(kernel_fn, scratch, qb_t, kb, vb_t, *scalars):
    B, H, nq, _, tq = qb_t.shape
    nck, tk = vb_t.shape[2], vb_t.shape[4]
    S = nck * tk
    assert nck & (nck - 1) == 0 and nq & (nq - 1) == 0
    return pl.pallas_call(
        kernel_fn,
        grid=(B, H),
        in_specs=[pl.BlockSpec(memory_space=pltpu.SMEM)] * len(scalars) + [
            pl.BlockSpec((None, None, nq, QK_PAD_B, tq), lambda b, h: (b, h, 0, 0, 0)),
            pl.BlockSpec((None, None, S, QK_PAD_B), lambda b, h: (b, h, 0, 0)),
            pl.BlockSpec((None, None, nck, V_DIM_B, tk), lambda b, h: (b, h, 0, 0, 0)),
        ],
        out_specs=pl.BlockSpec((None, None, nq, V_DIM_B, tq), lambda b, h: (b, h, 0, 0, 0)),
        out_shape=jax.ShapeDtypeStruct((B, H, nq, V_DIM_B, tq), BF16),
        scratch_shapes=scratch(tk, tq),
        compiler_params=_cparams(("arbitrary", "arbitrary")),
        name=kernel_fn.__name__.strip("_").replace("_kernel", ""),
    )(*scalars, qb_t, kb, vb_t)


def _mla_online_scratch(tk, tq):
    return [pltpu.VMEM((1, tq), F32), pltpu.VMEM((V_DIM_B + ONES_ROWS, tq), F32)]


def _mla_bounded_scratch(tk, tq):
    return [pltpu.VMEM((2, tk, tq), BF16)]


def _mla(qb_t, kb, vb_t, score_bound):
    shift = score_bound.reshape(1, 1).astype(F32)
    return lax.cond(
        2.0 * score_bound <= SOFTMAX_MAX_SHIFT_RANGE,
        lambda: _mla_call(_mla_bounded_kernel, _mla_bounded_scratch, qb_t, kb, vb_t, shift),
        lambda: _mla_call(_mla_online_kernel, _mla_online_scratch, qb_t, kb, vb_t),
    )


def _rope_table(S):
    pos = jnp.arange(S, dtype=F32)

    def cs(d):
        inv = ROPE_THETA ** (-jnp.arange(0, d, 2, dtype=F32) / d)
        ang = pos[:, None] * inv[None, :]
        return jnp.cos(ang).T, jnp.sin(ang).T

    ca, sa = cs(HEAD_DIM_A)
    cb, sb = cs(ROPE_B)
    return jnp.concatenate([ca, sa, cb, sb], axis=0)


def _gain_tile(swa_q_norm, swa_k_norm, mla_q_lora_norm, mla_kv_lora_norm, mla_q_norm, mla_k_norm):
    rows = jnp.concatenate([
        swa_q_norm, swa_k_norm, mla_q_lora_norm, mla_kv_lora_norm,
        mla_q_norm, mla_k_norm[:NOPE_B], mla_k_norm[NOPE_B:],
    ]).astype(F32)
    return jnp.broadcast_to(rows[:, None], (G_ROWS, PROJ_SUB))


def _score_bound(dim, q_scale, q_gain, k_gain):
    return (BF16_ROUND_MARGIN * dim * q_scale * jnp.max(jnp.abs(q_gain)) * jnp.max(jnp.abs(k_gain))).astype(F32)


def kernel(x_prompt, x_sample, c_prompt, c_sample, ada_w, ada_b, ffn1_norm, ffn1_wg, ffn1_wu, ffn1_wd, mix_norm, w_in, swa_q_norm, swa_k_norm, swa_sink, mla_q_lora_norm, mla_w_uq, mla_kv_lora_norm, mla_w_ukv, mla_q_norm, mla_k_norm, w_out, ffn2_norm, ffn2_wg, ffn2_wu, ffn2_wd, final_norm):
    assert ada_w.shape[0] == 1, "single-layer trunk"
    bp, bs = x_prompt.shape[0], x_sample.shape[0]
    assert bp + bs <= MOD_ROWS

    f1 = (ffn1_wg[0].astype(BF16), ffn1_wu[0].astype(BF16), ffn1_wd[0].astype(BF16))
    f2 = (ffn2_wg[0].astype(BF16), ffn2_wu[0].astype(BF16), ffn2_wd[0].astype(BF16))
    w1t = w_in[0].T.astype(BF16)
    wuq = mla_w_uq[0].reshape(Q_LORA, N_HEADS_B, QK_DIM_B)
    wuq = jnp.pad(wuq, ((0, 0), (0, 0), (0, QK_PAD_B - QK_DIM_B))).reshape(Q_LORA, N_HEADS_B * QK_PAD_B)
    wuqt = wuq.T.astype(BF16)
    wukvt = mla_w_ukv[0].T.astype(BF16)
    woa = w_out[0, :A_Q].astype(BF16)
    wob = w_out[0, A_Q:].astype(BF16)
    gains = _gain_tile(swa_q_norm[0], swa_k_norm[0], mla_q_lora_norm[0], mla_kv_lora_norm[0],
                       mla_q_norm[0], mla_k_norm[0])
    sink_rows = jnp.repeat(swa_sink[0].astype(F32) * LOG2E, LANE).reshape(N_KV_A, 1, GROUP_A * LANE)
    row = lambda v: v.reshape(1, D_MODEL)
    bound_a = _score_bound(HEAD_DIM_A, SWA_Q_SCALE, swa_q_norm[0], swa_k_norm[0])
    bound_b = _score_bound(QK_DIM_B, MLA_Q_SCALE, mla_q_norm[0], mla_k_norm[0])

    table = _rope_table(max(x_prompt.shape[1], x_sample.shape[1]))
    c_all = jnp.concatenate([c_prompt, c_sample, jnp.zeros((MOD_ROWS - bp - bs, D_MODEL), F32)], axis=0)
    mod = _modulation(c_all, ada_w[0], ada_b[0].reshape(1, -1)).reshape(MOD_ROWS, N_MOD, 1, D_MODEL)

    def trunk(x, m):
        sh1, sc1, g1, sh2, sc2, g2, sh3, sc3, g3 = (m[:, i] for i in range(N_MOD))
        S = x.shape[1]
        x1 = _ffn1(x, sh1, sc1, g1, row(ffn1_norm[0]), *f1)
        qa_t, ka, va_t, qb_t, kb, vb_t = _inproj(x1, sh2, sc2, row(mix_norm[0]), w1t, wuqt, wukvt,
                                                  gains, table)
        oa_t = _swa(qa_t, ka, va_t, sink_rows, bound_a)
        ob_t = _mla(qb_t, kb, vb_t, bound_b)
        return _ffn2(x1, oa_t.reshape(-1, A_Q, S), ob_t, woa, wob, g2, sh3, sc3, g3,
                     row(ffn2_norm[0]), *f2, row(final_norm[0]))

    y_prompt = trunk(x_prompt, mod[:bp])
    y_sample = trunk(x_sample, mod[bp:bp + bs])
    return (y_prompt, y_sample)
```

```python
import functools
import math

import jax
import jax.numpy as jnp
from jax import lax
from jax.experimental import pallas as pl
from jax.experimental.pallas import tpu as pltpu

F32 = jnp.float32
BF16 = jnp.bfloat16

D_MODEL = 1024
D_FF = 2816
N_HEADS_A = 8
N_KV_A = 2
HEAD_DIM_A = 64
GROUP_A = N_HEADS_A // N_KV_A
WINDOW = 128
N_HEADS_B = 8
Q_LORA = 256
KV_LORA = 128
NOPE_B = 64
ROPE_B = 32
QK_DIM_B = NOPE_B + ROPE_B
V_DIM_B = 64
ROPE_THETA = 10000.0
EPS = 1e-6
N_MOD = 9
A_Q = N_HEADS_A * HEAD_DIM_A
A_KV = N_KV_A * HEAD_DIM_A
IN_WIDTH = A_Q + 2 * A_KV + Q_LORA + KV_LORA + ROPE_B
NEG_BIG = -1e30
LOG2E = math.log2(math.e)
SWA_Q_SCALE = LOG2E / math.sqrt(HEAD_DIM_A)
MLA_Q_SCALE = LOG2E / math.sqrt(QK_DIM_B)
SOFTMAX_MAX_SHIFT_RANGE = 100.0
BF16_ROUND_MARGIN = 1.02

LANE = 128
QK_PAD_B = LANE
VMEM_LIMIT_BYTES = 56 * 1024 * 1024

MOD_ROWS = 16
MOD_BN = 1152
FFN_TM = 1024
FFN_FC = 256
FFN_NC = D_FF // FFN_FC
PROJ_TM = 1024
PROJ_SUB = 512
SWA_TQ = 1024
MLA_TQ = 512
MLA_MAX_UNROLL = 16
MLA_TK = 2048
ONES_ROWS = 16
PV_LHS_ROWS = 128

R_QA, R_KA, R_VA = 0, A_Q, A_Q + A_KV
R_CQ = A_Q + 2 * A_KV
R_CKV = R_CQ + Q_LORA
R_KR = R_CKV + KV_LORA
G_QA, G_KA = 0, HEAD_DIM_A
G_QL = 2 * HEAD_DIM_A
G_KVL = G_QL + Q_LORA
G_MQ = G_KVL + KV_LORA
G_KN = G_MQ + QK_DIM_B
G_KR = G_KN + NOPE_B
G_ROWS = G_KR + ROPE_B
T_CA, T_SA = 0, HEAD_DIM_A // 2
T_CB = HEAD_DIM_A
T_SB = T_CB + ROPE_B // 2
T_ROWS = T_SB + ROPE_B // 2

_NT = (((1,), (1,)), ((), ()))
_TN = (((0,), (0,)), ((), ()))


def _cparams(sem):
    return pltpu.CompilerParams(dimension_semantics=sem, vmem_limit_bytes=VMEM_LIMIT_BYTES)


def _const_spec(shape):
    n = len(shape)
    return pl.BlockSpec(shape, lambda *_: (0,) * n, pipeline_mode=pl.Buffered(1))


def _silu(x):
    return x * (1.0 / (1.0 + jnp.exp(-x)))


def _rms_rows(x):
    return lax.rsqrt(jnp.mean(x * x, axis=-1, keepdims=True) + EPS)


def _mod_kernel(c_ref, w_ref, b_ref, o_ref):
    a = _silu(c_ref[...]).astype(BF16)
    o_ref[...] = jnp.dot(a, w_ref[...].astype(BF16), preferred_element_type=F32) + b_ref[...]


def _modulation(c, ada_w, ada_b):
    n = ada_w.shape[1]
    return pl.pallas_call(
        _mod_kernel,
        grid=(n // MOD_BN,),
        in_specs=[
            pl.BlockSpec((MOD_ROWS, D_MODEL), lambda j: (0, 0)),
            pl.BlockSpec((D_MODEL, MOD_BN), lambda j: (0, j)),
            pl.BlockSpec((1, MOD_BN), lambda j: (0, j)),
        ],
        out_specs=pl.BlockSpec((MOD_ROWS, MOD_BN), lambda j: (0, j)),
        out_shape=jax.ShapeDtypeStruct((MOD_ROWS, n), F32),
        compiler_params=_cparams(("arbitrary",)),
        name="mod",
    )(c, ada_w, ada_b)


def _swiglu_into(acc_ref, h_ref, wg_ref, wu_ref, wd_ref):
    for c in range(FFN_NC):
        cols = slice(c * FFN_FC, (c + 1) * FFN_FC)
        h = h_ref[...]
        g = jnp.dot(h, wg_ref[:, cols], preferred_element_type=F32)
        u = jnp.dot(h, wu_ref[:, cols], preferred_element_type=F32)
        a = (_silu(g) * u).astype(BF16)
        y = jnp.dot(a, wd_ref[cols, :], preferred_element_type=F32)
        if c == 0:
            acc_ref[...] = y
        else:
            acc_ref[...] += y


def _ffn1_kernel(x_ref, sh_ref, sc_ref, gt_ref, nw_ref, wg_ref, wu_ref, wd_ref, o_ref, h_ref, acc_ref):
    x = x_ref[...]
    h_ref[...] = (x * _rms_rows(x) * (nw_ref[...] * (1.0 + sc_ref[...])) + sh_ref[...]).astype(BF16)
    _swiglu_into(acc_ref, h_ref, wg_ref, wu_ref, wd_ref)
    o_ref[...] = x_ref[...] + (0.5 * gt_ref[...]) * acc_ref[...]


def _ffn2_kernel(x_ref, oa_ref, ob_ref, woa_ref, wob_ref, g2_ref, sh_ref, sc_ref, gt_ref, nw_ref,
                 wg_ref, wu_ref, wd_ref, fw_ref, o_ref, h_ref, acc_ref, x2_ref):
    for qb in range(ob_ref.shape[1]):
        rows = slice(qb * MLA_TQ, (qb + 1) * MLA_TQ)
        y = lax.dot_general(oa_ref[:, rows], woa_ref[...], _TN, preferred_element_type=F32)
        ob = ob_ref[:, qb].reshape(N_HEADS_B * V_DIM_B, MLA_TQ)
        y = y + lax.dot_general(ob, wob_ref[...], _TN, preferred_element_type=F32)
        x2_ref[rows, :] = x_ref[rows, :] + g2_ref[...] * y
    x = x2_ref[...]
    h_ref[...] = (x * _rms_rows(x) * (nw_ref[...] * (1.0 + sc_ref[...])) + sh_ref[...]).astype(BF16)
    _swiglu_into(acc_ref, h_ref, wg_ref, wu_ref, wd_ref)
    z = x2_ref[...] + (0.5 * gt_ref[...]) * acc_ref[...]
    o_ref[...] = z * _rms_rows(z) * fw_ref[...]


def _row_spec():
    return pl.BlockSpec((None, 1, D_MODEL), lambda b, t: (b, 0, 0))


def _ffn_weight_specs():
    return [_const_spec((D_MODEL, D_FF)), _const_spec((D_MODEL, D_FF)), _const_spec((D_FF, D_MODEL))]


def _ffn1(x, sh, sc, gt, nw, wg, wu, wd):
    B, S, _ = x.shape
    tm = FFN_TM
    tok = pl.BlockSpec((None, tm, D_MODEL), lambda b, t: (b, t, 0))
    return pl.pallas_call(
        _ffn1_kernel,
        grid=(B, S // tm),
        in_specs=[tok, _row_spec(), _row_spec(), _row_spec(), _const_spec((1, D_MODEL))] + _ffn_weight_specs(),
        out_specs=tok,
        out_shape=jax.ShapeDtypeStruct(x.shape, F32),
        scratch_shapes=[pltpu.VMEM((tm, D_MODEL), BF16), pltpu.VMEM((tm, D_MODEL), F32)],
        compiler_params=_cparams(("arbitrary", "arbitrary")),
        name="ffn1",
    )(x, sh, sc, gt, nw, wg, wu, wd)


def _ffn2(x, oa_t, ob_t, woa, wob, g2, sh, sc, gt, nw, wg, wu, wd, fw):
    B, S, _ = x.shape
    tm = FFN_TM
    assert tm % MLA_TQ == 0
    tok = pl.BlockSpec((None, tm, D_MODEL), lambda b, t: (b, t, 0))
    att_a = pl.BlockSpec((None, A_Q, tm), lambda b, t: (b, 0, t))
    att_b = pl.BlockSpec((None, N_HEADS_B, tm // MLA_TQ, V_DIM_B, MLA_TQ), lambda b, t: (b, 0, t, 0, 0))
    return pl.pallas_call(
        _ffn2_kernel,
        grid=(B, S // tm),
        in_specs=[tok, att_a, att_b, _const_spec((A_Q, D_MODEL)), _const_spec((A_Q, D_MODEL)),
                  _row_spec(), _row_spec(), _row_spec(), _row_spec(), _const_spec((1, D_MODEL))]
                 + _ffn_weight_specs() + [_const_spec((1, D_MODEL))],
        out_specs=tok,
        out_shape=jax.ShapeDtypeStruct(x.shape, F32),
        scratch_shapes=[pltpu.VMEM((tm, D_MODEL), BF16), pltpu.VMEM((tm, D_MODEL), F32),
                        pltpu.VMEM((tm, D_MODEL), F32)],
        compiler_params=_cparams(("arbitrary", "arbitrary")),
        name="ffn2",
    )(x, oa_t, ob_t, woa, wob, g2, sh, sc, gt, nw, wg, wu, wd, fw)


def _rms_cols(x, n):
    return lax.rsqrt(jnp.sum(x * x, axis=0, keepdims=True) * (1.0 / n) + EPS)


def _rope_cols(y, cos, sin):
    half = y.shape[0] // 2
    y1, y2 = y[:half], y[half:]
    return jnp.concatenate([y1 * cos - y2 * sin, y2 * cos + y1 * sin], axis=0)


def _inproj_kernel(x_ref, sh_ref, sc_ref, nw_ref, w1_ref, wuq_ref, wukv_ref, g_ref, t_ref,
                   qa_ref, ka_ref, va_ref, qb_ref, kb_ref, vb_ref,
                   zt_ref, cq_ref, ckv_ref, qbt_ref, kvt_ref):
    n = PROJ_SUB

    def tables(tok):
        return (t_ref[T_CA:T_CA + 32, tok], t_ref[T_SA:T_SA + 32, tok],
                t_ref[T_CB:T_CB + 16, tok], t_ref[T_SB:T_SB + 16, tok])

    def project(sub):
        x = x_ref[sub * n:(sub + 1) * n, :]
        h = (x * _rms_rows(x) * (nw_ref[...] * (1.0 + sc_ref[...])) + sh_ref[...]).astype(BF16)
        zt_ref[sub] = lax.dot_general(w1_ref[...], h, _NT, preferred_element_type=F32)

    def swa_heads(sub):
        off = sub * n
        tok = slice(off, off + n)
        cos_a, sin_a, _, _ = tables(tok)
        zt = zt_ref.at[sub]
        g_qa = g_ref[G_QA:G_QA + HEAD_DIM_A, :]
        for hd in range(N_HEADS_A):
            blk = zt[R_QA + hd * HEAD_DIM_A:R_QA + (hd + 1) * HEAD_DIM_A, :]
            r = _rms_cols(blk, HEAD_DIM_A) * SWA_Q_SCALE
            qa_ref[hd, :, tok] = (_rope_cols(blk * g_qa, cos_a, sin_a) * r).astype(BF16)
        g_ka = g_ref[G_KA:G_KA + HEAD_DIM_A, :]
        ks = []
        for hd in range(N_KV_A):
            blk = zt[R_KA + hd * HEAD_DIM_A:R_KA + (hd + 1) * HEAD_DIM_A, :]
            ks.append(_rope_cols(blk * g_ka, cos_a, sin_a) * _rms_cols(blk, HEAD_DIM_A))
        ka_ref[tok, :] = jnp.concatenate(ks, axis=0).T.astype(BF16)
        vt = zt[R_VA:R_VA + A_KV, :].astype(BF16)
        for c in range(n // LANE):
            va_ref[off // LANE + c] = vt[:, c * LANE:(c + 1) * LANE]
        cq = zt[R_CQ:R_CQ + Q_LORA, :]
        cq_ref[sub] = (cq * g_ref[G_QL:G_QL + Q_LORA, :] * _rms_cols(cq, Q_LORA)).astype(BF16)
        ckv = zt[R_CKV:R_CKV + KV_LORA, :]
        ckv_ref[sub] = (ckv * g_ref[G_KVL:G_KVL + KV_LORA, :] * _rms_cols(ckv, KV_LORA)).astype(BF16)

    def up_project(sub):
        qbt_ref[sub] = jnp.dot(wuq_ref[...], cq_ref[sub], preferred_element_type=F32)
        kvt_ref[sub] = jnp.dot(wukv_ref[...], ckv_ref[sub], preferred_element_type=F32)

    def mla_heads(sub):
        off = sub * n
        tok = slice(off, off + n)
        _, _, cos_b, sin_b = tables(tok)
        g_mq = g_ref[G_MQ:G_MQ + QK_DIM_B, :]
        pad = jnp.zeros((QK_PAD_B - QK_DIM_B, n), BF16)
        for hd in range(N_HEADS_B):
            blk = qbt_ref[sub, hd * QK_PAD_B:hd * QK_PAD_B + QK_DIM_B, :]
            y = blk * g_mq
            o = jnp.concatenate([y[:NOPE_B], _rope_cols(y[NOPE_B:], cos_b, sin_b)], axis=0)
            o = (o * (_rms_cols(blk, QK_DIM_B) * MLA_Q_SCALE)).astype(BF16)
            step = min(n, MLA_TQ)
            for c in range(n // step):
                lo = off + c * step
                qb_ref[hd, lo // MLA_TQ, :, lo % MLA_TQ:lo % MLA_TQ + step] = (
                    jnp.concatenate([o[:, c * step:(c + 1) * step], pad[:, :step]], axis=0))
        kr = zt_ref[sub, R_KR:R_KR + ROPE_B, :]
        ss_kr = jnp.sum(kr * kr, axis=0, keepdims=True)
        kr_rot = _rope_cols(kr * g_ref[G_KR:G_KR + ROPE_B, :], cos_b, sin_b)
        g_kn = g_ref[G_KN:G_KN + NOPE_B, :]
        padf = jnp.zeros((QK_PAD_B - QK_DIM_B, n), F32)
        for hd in range(N_HEADS_B):
            kn = kvt_ref[sub, hd * 2 * NOPE_B:hd * 2 * NOPE_B + NOPE_B, :]
            r = lax.rsqrt((jnp.sum(kn * kn, axis=0, keepdims=True) + ss_kr) * (1.0 / QK_DIM_B) + EPS)
            kt = jnp.concatenate([kn * g_kn * r, kr_rot * r, padf], axis=0)
            kb_ref[hd, tok, :] = kt.T.astype(BF16)
            v = kvt_ref[sub, hd * 2 * NOPE_B + NOPE_B:(hd + 1) * 2 * NOPE_B, :].astype(BF16)
            vb_ref[hd, :, tok] = v

    assert PROJ_TM == 2 * PROJ_SUB
    project(0)
    project(1)
    swa_heads(0)
    up_project(0)
    swa_heads(1)
    mla_heads(0)
    up_project(1)
    mla_heads(1)


def _inproj(x, sh, sc, nw, w1t, wuqt, wukvt, gains, table):
    B, S, _ = x.shape
    tm = PROJ_TM
    nt = S // tm
    assert MLA_TK % tm == 0
    tiles_per_chunk = MLA_TK // tm
    out_shape = [
        jax.ShapeDtypeStruct((B, N_HEADS_A, HEAD_DIM_A, S), BF16),
        jax.ShapeDtypeStruct((B, S, A_KV), BF16),
        jax.ShapeDtypeStruct((B, S // LANE, A_KV, LANE), BF16),
        jax.ShapeDtypeStruct((B, N_HEADS_B, S // MLA_TQ, QK_PAD_B, MLA_TQ), BF16),
        jax.ShapeDtypeStruct((B, N_HEADS_B, S, QK_PAD_B), BF16),
        jax.ShapeDtypeStruct((B, N_HEADS_B, S // MLA_TK, V_DIM_B, MLA_TK), BF16),
    ]
    out_specs = [
        pl.BlockSpec((None, N_HEADS_A, HEAD_DIM_A, tm), lambda b, t: (b, 0, 0, t)),
        pl.BlockSpec((None, tm, A_KV), lambda b, t: (b, t, 0)),
        pl.BlockSpec((None, tm // LANE, A_KV, LANE), lambda b, t: (b, t, 0, 0)),
        pl.BlockSpec((None, N_HEADS_B, tm // MLA_TQ, QK_PAD_B, MLA_TQ), lambda b, t: (b, 0, t, 0, 0)),
        pl.BlockSpec((None, N_HEADS_B, tm, QK_PAD_B), lambda b, t: (b, 0, t, 0)),
        pl.BlockSpec((None, N_HEADS_B, None, V_DIM_B, tm), lambda b, t: (b, 0, t // tiles_per_chunk, 0, t % tiles_per_chunk)),
    ]
    return pl.pallas_call(
        _inproj_kernel,
        grid=(B, nt),
        in_specs=[
            pl.BlockSpec((None, tm, D_MODEL), lambda b, t: (b, t, 0)),
            _row_spec(), _row_spec(), _const_spec((1, D_MODEL)),
            _const_spec((IN_WIDTH, D_MODEL)),
            _const_spec((N_HEADS_B * QK_PAD_B, Q_LORA)),
            _const_spec((N_HEADS_B * 2 * NOPE_B, KV_LORA)),
            _const_spec((G_ROWS, PROJ_SUB)),
            pl.BlockSpec((T_ROWS, tm), lambda b, t: (0, t)),
        ],
        out_specs=out_specs,
        out_shape=out_shape,
        scratch_shapes=[
            pltpu.VMEM((2, IN_WIDTH, PROJ_SUB), F32),
            pltpu.VMEM((2, Q_LORA, PROJ_SUB), BF16),
            pltpu.VMEM((2, KV_LORA, PROJ_SUB), BF16),
            pltpu.VMEM((2, N_HEADS_B * QK_PAD_B, PROJ_SUB), F32),
            pltpu.VMEM((2, N_HEADS_B * 2 * NOPE_B, PROJ_SUB), F32),
        ],
        compiler_params=_cparams(("arbitrary", "arbitrary")),
        name="inproj",
    )(x, sh, sc, nw, w1t, wuqt, wukvt, gains, table)


def _swa_kernel(*refs, n_blocks, bounded):
    if bounded:
        shift_ref, q_ref, k_ref, v_ref, sink_ref, o_ref, p_ref = refs
    else:
        q_ref, k_ref, v_ref, sink_ref, o_ref, p_ref, m_ref = refs
    tq = q_ref.shape[-1]
    nq = tq // LANE
    wide = GROUP_A * LANE
    row = lax.broadcasted_iota(jnp.int32, (LANE, wide), 0)
    col = lax.broadcasted_iota(jnp.int32, (LANE, wide), 1) & (LANE - 1)
    diff = row - col
    zeros = jnp.zeros((HEAD_DIM_A, LANE), BF16)
    ones = jnp.ones((ONES_ROWS, LANE), BF16)
    chains = [(i, j) for i in range(nq) for j in range(N_KV_A)]

    def blocks(i):
        nb = pl.program_id(1) * nq + i
        kb_idx = [jnp.maximum(nb - 1, 0), nb, jnp.minimum(nb + 1, n_blocks - 1)]
        lo = jnp.where(nb > 0, 0, 2 * LANE)
        hi = jnp.where(nb < n_blocks - 1, 0, -2 * LANE)
        return kb_idx, [diff >= lo, None, diff <= hi]

    def probs(i, j, slot):
        kb_idx, masks = blocks(i)
        cols = []
        for g in range(GROUP_A):
            qh = q_ref[j * GROUP_A + g, :, i * LANE:(i + 1) * LANE]
            cols.append(jnp.concatenate([qh, zeros] if j == 0 else [zeros, qh], axis=0))
        qp = jnp.concatenate(cols, axis=1)
        k3 = jnp.concatenate([k_ref[pl.ds(pl.multiple_of(kb_idx[c] * LANE, LANE), LANE), :] for c in range(3)],
                             axis=0)
        s3 = jnp.dot(k3, qp, preferred_element_type=F32)
        ss = [s3[c * LANE:(c + 1) * LANE] for c in range(3)]
        if bounded:
            m = shift_ref[0, 0]
        else:
            m = sink_ref[j]
            for c in range(3):
                sc = ss[c] if masks[c] is None else jnp.where(masks[c], ss[c], NEG_BIG)
                m = jnp.maximum(m, jnp.max(sc, axis=0, keepdims=True))
            m_ref[slot] = m
        for c in range(3):
            p = jnp.exp2(ss[c] - m)
            if masks[c] is not None:
                p = jnp.where(masks[c], p, 0.0)
            p_ref[slot, c] = p.astype(BF16)

    def apply(i, j, slot):
        kb_idx, _ = blocks(i)
        m = shift_ref[0, 0] if bounded else m_ref[slot]
        vt = jnp.concatenate([v_ref[kb_idx[c], j * HEAD_DIM_A:(j + 1) * HEAD_DIM_A, :] for c in range(3)], axis=1)
        v1 = jnp.concatenate([vt, jnp.ones((ONES_ROWS, 3 * LANE), BF16)], axis=0)
        acc = jnp.dot(v1, p_ref[slot].reshape(3 * LANE, wide), preferred_element_type=F32)
        l = acc[HEAD_DIM_A:HEAD_DIM_A + 1] + jnp.exp2(sink_ref[j] - m)
        o = acc[:HEAD_DIM_A] * (1.0 / l)
        for g in range(GROUP_A):
            o_ref[j * GROUP_A + g, :, i * LANE:(i + 1) * LANE] = o[:, g * LANE:(g + 1) * LANE].astype(BF16)

    probs(*chains[0], 0)
    for c, (i, j) in enumerate(chains):
        if c + 1 < len(chains):
            probs(*chains[c + 1], (c + 1) % 2)
        apply(i, j, c % 2)


def _swa_call(bounded, qa_t, ka, va_t, sink_rows, *scalars):
    B, _, _, S = qa_t.shape
    tq = SWA_TQ
    nbk = S // LANE
    qspec = pl.BlockSpec((None, N_HEADS_A, HEAD_DIM_A, tq), lambda b, t: (b, 0, 0, t))
    return pl.pallas_call(
        functools.partial(_swa_kernel, n_blocks=nbk, bounded=bounded),
        grid=(B, S // tq),
        in_specs=[pl.BlockSpec(memory_space=pltpu.SMEM)] * len(scalars) + [
            qspec,
            pl.BlockSpec((None, S, A_KV), lambda b, t: (b, 0, 0)),
            pl.BlockSpec((None, nbk, A_KV, LANE), lambda b, t: (b, 0, 0, 0)),
            _const_spec((N_KV_A, 1, GROUP_A * LANE)),
        ],
        out_specs=qspec,
        out_shape=jax.ShapeDtypeStruct(qa_t.shape, BF16),
        scratch_shapes=[pltpu.VMEM((2, 3, LANE, GROUP_A * LANE), BF16)]
                       + ([] if bounded else [pltpu.VMEM((2, 1, GROUP_A * LANE), F32)]),
        compiler_params=_cparams(("arbitrary", "arbitrary")),
        name="swa_bounded" if bounded else "swa_online",
    )(*scalars, qa_t, ka, va_t, sink_rows)


def _swa(qa_t, ka, va_t, sink_rows, score_bound):
    shift = jnp.maximum(score_bound, jnp.max(sink_rows))
    return lax.cond(
        shift + score_bound <= SOFTMAX_MAX_SHIFT_RANGE,
        lambda: _swa_call(True, qa_t, ka, va_t, sink_rows, shift.reshape(1, 1).astype(F32)),
        lambda: _swa_call(False, qa_t, ka, va_t, sink_rows),
    )


def _mla_online_kernel(q_ref, k_ref, v_ref, o_ref, m_ref, acc_ref):
    nq = q_ref.shape[0]
    n = v_ref.shape[0]
    tk = v_ref.shape[-1]
    ones = jnp.ones((ONES_ROWS, tk), BF16)

    def qblock(qb, carry):
        m_ref[...] = jnp.full_like(m_ref, NEG_BIG)
        acc_ref[...] = jnp.zeros_like(acc_ref)

        def chunk(c, carry2):
            k = k_ref[pl.ds(pl.multiple_of(c * tk, tk), tk), :]
            s = jnp.dot(k, q_ref[qb], preferred_element_type=F32)
            m_old = m_ref[...]
            m_new = jnp.maximum(m_old, jnp.max(s, axis=0, keepdims=True))
            p = jnp.exp2(s - m_new).astype(BF16)
            v1 = jnp.concatenate([v_ref[c], ones], axis=0)
            acc_ref[...] = jnp.exp2(m_old - m_new) * acc_ref[...] + jnp.dot(v1, p, preferred_element_type=F32)
            m_ref[...] = m_new
            return carry2

        lax.fori_loop(0, n, chunk, 0)
        acc = acc_ref[...]
        o_ref[qb] = (acc[:V_DIM_B] * (1.0 / acc[V_DIM_B:V_DIM_B + 1])).astype(BF16)
        return carry

    lax.fori_loop(0, nq, qblock, 0)


def _mla_bounded_kernel(shift_ref, q_ref, k_ref, v_ref, o_ref, p_ref):
    nq = q_ref.shape[0]
    n = v_ref.shape[0]
    tk = v_ref.shape[-1]
    total = nq * n
    log2n = n.bit_length() - 1
    qpb = min(nq, max(1, MLA_MAX_UNROLL // n))
    shift = shift_ref[0, 0]
    ones = jnp.ones((PV_LHS_ROWS - V_DIM_B, tk), BF16)

    def probs(t, slot):
        c = t & (n - 1)
        k = k_ref[pl.ds(pl.multiple_of(c * tk, tk), tk), :]
        s = jnp.dot(k, q_ref[t >> log2n], preferred_element_type=F32)
        p_ref[slot] = jnp.exp2(s - shift).astype(BF16)

    def pv(c, slot):
        v1 = jnp.concatenate([v_ref[c], ones], axis=0)
        return jnp.dot(v1, p_ref[slot], preferred_element_type=F32)

    probs(0, 0)

    def body(j, carry):
        t0 = j * (qpb * n)
        for u in range(qpb * n):
            probs(jnp.minimum(t0 + u + 1, total - 1), (u + 1) % 2)
            part = pv(u % n, u % 2)
            acc = part if u % n == 0 else acc + part
            if u % n == n - 1:
                o_ref[j * qpb + u // n] = (acc[:V_DIM_B] * (1.0 / acc[V_DIM_B:V_DIM_B + 1])).astype(BF16)
        return carry

    lax.fori_loop(0, nq // qpb, body, 0)


def _mla_call(kernel_fn, scratch, qb_t, kb, vb_t, *scalars):
    B, H, nq, _, tq = qb_t.shape
    nck, tk = vb_t.shape[2], vb_t.shape[4]
    S = nck * tk
    assert nck & (nck - 1) == 0 and nq & (nq - 1) == 0
    return pl.pallas_call(
        kernel_fn,
        grid=(B, H),
        in_specs=[pl.BlockSpec(memory_space=pltpu.SMEM)] * len(scalars) + [
            pl.BlockSpec((None, None, nq, QK_PAD_B, tq), lambda b, h: (b, h, 0, 0, 0)),
            pl.BlockSpec((None, None, S, QK_PAD_B), lambda b, h: (b, h, 0, 0)),
            pl.BlockSpec((None, None, nck, V_DIM_B, tk), lambda b, h: (b, h, 0, 0, 0)),
        ],
        out_specs=pl.BlockSpec((None, None, nq, V_DIM_B, tq), lambda b, h: (b, h, 0, 0, 0)),
        out_shape=jax.ShapeDtypeStruct((B, H, nq, V_DIM_B, tq), BF16),
        scratch_shapes=scratch(tk, tq),
        compiler_params=_cparams(("arbitrary", "arbitrary")),
        name=kernel_fn.__name__.strip("_").replace("_kernel", ""),
    )(*scalars, qb_t, kb, vb_t)


def _mla_online_scratch(tk, tq):
    return [pltpu.VMEM((1, tq), F32), pltpu.VMEM((V_DIM_B + ONES_ROWS, tq), F32)]


def _mla_bounded_scratch(tk, tq):
    return [pltpu.VMEM((2, tk, tq), BF16)]


def _mla(qb_t, kb, vb_t, score_bound):
    shift = score_bound.reshape(1, 1).astype(F32)
    return lax.cond(
        2.0 * score_bound <= SOFTMAX_MAX_SHIFT_RANGE,
        lambda: _mla_call(_mla_bounded_kernel, _mla_bounded_scratch, qb_t, kb, vb_t, shift),
        lambda: _mla_call(_mla_online_kernel, _mla_online_scratch, qb_t, kb, vb_t),
    )


def _rope_table(S):
    pos = jnp.arange(S, dtype=F32)

    def cs(d):
        inv = ROPE_THETA ** (-jnp.arange(0, d, 2, dtype=F32) / d)
        ang = pos[:, None] * inv[None, :]
        return jnp.cos(ang).T, jnp.sin(ang).T

    ca, sa = cs(HEAD_DIM_A)
    cb, sb = cs(ROPE_B)
    return jnp.concatenate([ca, sa, cb, sb], axis=0)


def _gain_tile(swa_q_norm, swa_k_norm, mla_q_lora_norm, mla_kv_lora_norm, mla_q_norm, mla_k_norm):
    rows = jnp.concatenate([
        swa_q_norm, swa_k_norm, mla_q_lora_norm, mla_kv_lora_norm,
        mla_q_norm, mla_k_norm[:NOPE_B], mla_k_norm[NOPE_B:],
    ]).astype(F32)
    return jnp.broadcast_to(rows[:, None], (G_ROWS, PROJ_SUB))


def _score_bound(dim, q_scale, q_gain, k_gain):
    return (BF16_ROUND_MARGIN * dim * q_scale * jnp.max(jnp.abs(q_gain)) * jnp.max(jnp.abs(k_gain))).astype(F32)


def kernel(x_prompt, x_sample, c_prompt, c_sample, ada_w, ada_b, ffn1_norm, ffn1_wg, ffn1_wu, ffn1_wd, mix_norm, w_in, swa_q_norm, swa_k_norm, swa_sink, mla_q_lora_norm, mla_w_uq, mla_kv_lora_norm, mla_w_ukv, mla_q_norm, mla_k_norm, w_out, ffn2_norm, ffn2_wg, ffn2_wu, ffn2_wd, final_norm):
    assert ada_w.shape[0] == 1, "single-layer trunk"
    bp, bs = x_prompt.shape[0], x_sample.shape[0]
    assert bp + bs <= MOD_ROWS

    f1 = (ffn1_wg[0].astype(BF16), ffn1_wu[0].astype(BF16), ffn1_wd[0].astype(BF16))
    f2 = (ffn2_wg[0].astype(BF16), ffn2_wu[0].astype(BF16), ffn2_wd[0].astype(BF16))
    w1t = w_in[0].T.astype(BF16)
    wuq = mla_w_uq[0].reshape(Q_LORA, N_HEADS_B, QK_DIM_B)
    wuq = jnp.pad(wuq, ((0, 0), (0, 0), (0, QK_PAD_B - QK_DIM_B))).reshape(Q_LORA, N_HEADS_B * QK_PAD_B)
    wuqt = wuq.T.astype(BF16)
    wukvt = mla_w_ukv[0].T.astype(BF16)
    woa = w_out[0, :A_Q].astype(BF16)
    wob = w_out[0, A_Q:].astype(BF16)
    gains = _gain_tile(swa_q_norm[0], swa_k_norm[0], mla_q_lora_norm[0], mla_kv_lora_norm[0],
                       mla_q_norm[0], mla_k_norm[0])
    sink_rows = jnp.repeat(swa_sink[0].astype(F32) * LOG2E, LANE).reshape(N_KV_A, 1, GROUP_A * LANE)
    row = lambda v: v.reshape(1, D_MODEL)
    bound_a = _score_bound(HEAD_DIM_A, SWA_Q_SCALE, swa_q_norm[0], swa_k_norm[0])
    bound_b = _score_bound(QK_DIM_B, MLA_Q_SCALE, mla_q_norm[0], mla_k_norm[0])

    table = _rope_table(max(x_prompt.shape[1], x_sample.shape[1]))
    c_all = jnp.concatenate([c_prompt, c_sample, jnp.zeros((MOD_ROWS - bp - bs, D_MODEL), F32)], axis=0)
    mod = _modulation(c_all, ada_w[0], ada_b[0].reshape(1, -1)).reshape(MOD_ROWS, N_MOD, 1, D_MODEL)

    def trunk(x, m):
        sh1, sc1, g1, sh2, sc2, g2, sh3, sc3, g3 = (m[:, i] for i in range(N_MOD))
        S = x.shape[1]
        x1 = _ffn1(x, sh1, sc1, g1, row(ffn1_norm[0]), *f1)
        qa_t, ka, va_t, qb_t, kb, vb_t = _inproj(x1, sh2, sc2, row(mix_norm[0]), w1t, wuqt, wukvt,
                                                  gains, table)
        oa_t = _swa(qa_t, ka, va_t, sink_rows, bound_a)
        ob_t = _mla(qb_t, kb, vb_t, bound_b)
        return _ffn2(x1, oa_t.reshape(-1, A_Q, S), ob_t, woa, wob, g2, sh3, sc3, g3,
                     row(ffn2_norm[0]), *f2, row(final_norm[0]))

    y_prompt = trunk(x_prompt, mod[:bp])
    y_sample = trunk(x_sample, mod[bp:bp + bs])
    return (y_prompt, y_sample)
```

```python
import functools
import math

import jax
import jax.numpy as jnp
from jax import lax
from jax.experimental import pallas as pl
from jax.experimental.pallas import tpu as pltpu

F32 = jnp.float32
BF16 = jnp.bfloat16

D_MODEL = 1024
D_FF = 2816
N_HEADS_A = 8
N_KV_A = 2
HEAD_DIM_A = 64
GROUP_A = N_HEADS_A // N_KV_A
WINDOW = 128
N_HEADS_B = 8
Q_LORA = 256
KV_LORA = 128
NOPE_B = 64
ROPE_B = 32
QK_DIM_B = NOPE_B + ROPE_B
V_DIM_B = 64
ROPE_THETA = 10000.0
EPS = 1e-6
N_MOD = 9
A_Q = N_HEADS_A * HEAD_DIM_A
A_KV = N_KV_A * HEAD_DIM_A
IN_WIDTH = A_Q + 2 * A_KV + Q_LORA + KV_LORA + ROPE_B
NEG_BIG = -1e30
LOG2E = math.log2(math.e)
SWA_Q_SCALE = LOG2E / math.sqrt(HEAD_DIM_A)
MLA_Q_SCALE = LOG2E / math.sqrt(QK_DIM_B)
SOFTMAX_MAX_SHIFT_RANGE = 100.0
BF16_ROUND_MARGIN = 1.02

LANE = 128
QK_PAD_B = LANE
VMEM_LIMIT_BYTES = 56 * 1024 * 1024

MOD_ROWS = 16
MOD_BN = 1152
FFN_TM = 1024
FFN_FC = 256
FFN_NC = D_FF // FFN_FC
PROJ_TM = 1024
PROJ_SUB = 512
SWA_TQ = 2048
MLA_TQ = 512
MLA_MAX_UNROLL = 16
MLA_TK = 2048
ONES_ROWS = 16
PV_LHS_ROWS = 128

R_QA, R_KA, R_VA = 0, A_Q, A_Q + A_KV
R_CQ = A_Q + 2 * A_KV
R_CKV = R_CQ + Q_LORA
R_KR = R_CKV + KV_LORA
G_QA, G_KA = 0, HEAD_DIM_A
G_QL = 2 * HEAD_DIM_A
G_KVL = G_QL + Q_LORA
G_MQ = G_KVL + KV_LORA
G_KN = G_MQ + QK_DIM_B
G_KR = G_KN + NOPE_B
G_ROWS = G_KR + ROPE_B
T_CA, T_SA = 0, HEAD_DIM_A // 2
T_CB = HEAD_DIM_A
T_SB = T_CB + ROPE_B // 2
T_ROWS = T_SB + ROPE_B // 2

_NT = (((1,), (1,)), ((), ()))
_TN = (((0,), (0,)), ((), ()))


def _cparams(sem):
    return pltpu.CompilerParams(dimension_semantics=sem, vmem_limit_bytes=VMEM_LIMIT_BYTES)


def _const_spec(shape):
    n = len(shape)
    return pl.BlockSpec(shape, lambda *_: (0,) * n, pipeline_mode=pl.Buffered(1))


def _silu(x):
    return x * (1.0 / (1.0 + jnp.exp(-x)))


def _rms_rows(x):
    return lax.rsqrt(jnp.mean(x * x, axis=-1, keepdims=True) + EPS)


def _mod_kernel(c_ref, w_ref, b_ref, o_ref):
    a = _silu(c_ref[...]).astype(BF16)
    o_ref[...] = jnp.dot(a, w_ref[...].astype(BF16), preferred_element_type=F32) + b_ref[...]


def _modulation(c, ada_w, ada_b):
    n = ada_w.shape[1]
    return pl.pallas_call(
        _mod_kernel,
        grid=(n // MOD_BN,),
        in_specs=[
            pl.BlockSpec((MOD_ROWS, D_MODEL), lambda j: (0, 0)),
            pl.BlockSpec((D_MODEL, MOD_BN), lambda j: (0, j)),
            pl.BlockSpec((1, MOD_BN), lambda j: (0, j)),
        ],
        out_specs=pl.BlockSpec((MOD_ROWS, MOD_BN), lambda j: (0, j)),
        out_shape=jax.ShapeDtypeStruct((MOD_ROWS, n), F32),
        compiler_params=_cparams(("arbitrary",)),
        name="mod",
    )(c, ada_w, ada_b)


def _swiglu_into(acc_ref, h_ref, wg_ref, wu_ref, wd_ref):
    for c in range(FFN_NC):
        cols = slice(c * FFN_FC, (c + 1) * FFN_FC)
        h = h_ref[...]
        g = jnp.dot(h, wg_ref[:, cols], preferred_element_type=F32)
        u = jnp.dot(h, wu_ref[:, cols], preferred_element_type=F32)
        a = (_silu(g) * u).astype(BF16)
        y = jnp.dot(a, wd_ref[cols, :], preferred_element_type=F32)
        if c == 0:
            acc_ref[...] = y
        else:
            acc_ref[...] += y


def _ffn1_kernel(x_ref, sh_ref, sc_ref, gt_ref, nw_ref, wg_ref, wu_ref, wd_ref, o_ref, h_ref, acc_ref):
    x = x_ref[...]
    h_ref[...] = (x * _rms_rows(x) * (nw_ref[...] * (1.0 + sc_ref[...])) + sh_ref[...]).astype(BF16)
    _swiglu_into(acc_ref, h_ref, wg_ref, wu_ref, wd_ref)
    o_ref[...] = x_ref[...] + (0.5 * gt_ref[...]) * acc_ref[...]


def _ffn2_kernel(x_ref, oa_ref, ob_ref, woa_ref, wob_ref, g2_ref, sh_ref, sc_ref, gt_ref, nw_ref,
                 wg_ref, wu_ref, wd_ref, fw_ref, o_ref, h_ref, acc_ref, x2_ref):
    for qb in range(ob_ref.shape[1]):
        rows = slice(qb * MLA_TQ, (qb + 1) * MLA_TQ)
        y = lax.dot_general(oa_ref[:, rows], woa_ref[...], _TN, preferred_element_type=F32)
        ob = ob_ref[:, qb].reshape(N_HEADS_B * V_DIM_B, MLA_TQ)
        y = y + lax.dot_general(ob, wob_ref[...], _TN, preferred_element_type=F32)
        x2_ref[rows, :] = x_ref[rows, :] + g2_ref[...] * y
    x = x2_ref[...]
    h_ref[...] = (x * _rms_rows(x) * (nw_ref[...] * (1.0 + sc_ref[...])) + sh_ref[...]).astype(BF16)
    _swiglu_into(acc_ref, h_ref, wg_ref, wu_ref, wd_ref)
    z = x2_ref[...] + (0.5 * gt_ref[...]) * acc_ref[...]
    o_ref[...] = z * _rms_rows(z) * fw_ref[...]


def _row_spec():
    return pl.BlockSpec((None, 1, D_MODEL), lambda b, t: (b, 0, 0))


def _ffn_weight_specs():
    return [_const_spec((D_MODEL, D_FF)), _const_spec((D_MODEL, D_FF)), _const_spec((D_FF, D_MODEL))]


def _ffn1(x, sh, sc, gt, nw, wg, wu, wd):
    B, S, _ = x.shape
    tm = FFN_TM
    tok = pl.BlockSpec((None, tm, D_MODEL), lambda b, t: (b, t, 0))
    return pl.pallas_call(
        _ffn1_kernel,
        grid=(B, S // tm),
        in_specs=[tok, _row_spec(), _row_spec(), _row_spec(), _const_spec((1, D_MODEL))] + _ffn_weight_specs(),
        out_specs=tok,
        out_shape=jax.ShapeDtypeStruct(x.shape, F32),
        scratch_shapes=[pltpu.VMEM((tm, D_MODEL), BF16), pltpu.VMEM((tm, D_MODEL), F32)],
        compiler_params=_cparams(("arbitrary", "arbitrary")),
        name="ffn1",
    )(x, sh, sc, gt, nw, wg, wu, wd)


def _ffn2(x, oa_t, ob_t, woa, wob, g2, sh, sc, gt, nw, wg, wu, wd, fw):
    B, S, _ = x.shape
    tm = FFN_TM
    assert tm % MLA_TQ == 0
    tok = pl.BlockSpec((None, tm, D_MODEL), lambda b, t: (b, t, 0))
    att_a = pl.BlockSpec((None, A_Q, tm), lambda b, t: (b, 0, t))
    att_b = pl.BlockSpec((None, N_HEADS_B, tm // MLA_TQ, V_DIM_B, MLA_TQ), lambda b, t: (b, 0, t, 0, 0))
    return pl.pallas_call(
        _ffn2_kernel,
        grid=(B, S // tm),
        in_specs=[tok, att_a, att_b, _const_spec((A_Q, D_MODEL)), _const_spec((A_Q, D_MODEL)),
                  _row_spec(), _row_spec(), _row_spec(), _row_spec(), _const_spec((1, D_MODEL))]
                 + _ffn_weight_specs() + [_const_spec((1, D_MODEL))],
        out_specs=tok,
        out_shape=jax.ShapeDtypeStruct(x.shape, F32),
        scratch_shapes=[pltpu.VMEM((tm, D_MODEL), BF16), pltpu.VMEM((tm, D_MODEL), F32),
                        pltpu.VMEM((tm, D_MODEL), F32)],
        compiler_params=_cparams(("arbitrary", "arbitrary")),
        name="ffn2",
    )(x, oa_t, ob_t, woa, wob, g2, sh, sc, gt, nw, wg, wu, wd, fw)


def _rms_cols(x, n):
    return lax.rsqrt(jnp.sum(x * x, axis=0, keepdims=True) * (1.0 / n) + EPS)


def _rope_cols(y, cos, sin):
    half = y.shape[0] // 2
    y1, y2 = y[:half], y[half:]
    return jnp.concatenate([y1 * cos - y2 * sin, y2 * cos + y1 * sin], axis=0)


def _inproj_kernel(x_ref, sh_ref, sc_ref, nw_ref, w1_ref, wuq_ref, wukv_ref, g_ref, t_ref,
                   qa_ref, ka_ref, va_ref, qb_ref, kb_ref, vb_ref,
                   zt_ref, cq_ref, ckv_ref, qbt_ref, kvt_ref):
    n = PROJ_SUB

    def tables(tok):
        return (t_ref[T_CA:T_CA + 32, tok], t_ref[T_SA:T_SA + 32, tok],
                t_ref[T_CB:T_CB + 16, tok], t_ref[T_SB:T_SB + 16, tok])

    def project(sub):
        x = x_ref[sub * n:(sub + 1) * n, :]
        h = (x * _rms_rows(x) * (nw_ref[...] * (1.0 + sc_ref[...])) + sh_ref[...]).astype(BF16)
        zt_ref[sub] = lax.dot_general(w1_ref[...], h, _NT, preferred_element_type=F32)

    def swa_heads(sub):
        off = sub * n
        tok = slice(off, off + n)
        cos_a, sin_a, _, _ = tables(tok)
        zt = zt_ref.at[sub]
        g_qa = g_ref[G_QA:G_QA + HEAD_DIM_A, :]
        for hd in range(N_HEADS_A):
            blk = zt[R_QA + hd * HEAD_DIM_A:R_QA + (hd + 1) * HEAD_DIM_A, :]
            r = _rms_cols(blk, HEAD_DIM_A) * SWA_Q_SCALE
            qa_ref[hd, :, tok] = (_rope_cols(blk * g_qa, cos_a, sin_a) * r).astype(BF16)
        g_ka = g_ref[G_KA:G_KA + HEAD_DIM_A, :]
        ks = []
        for hd in range(N_KV_A):
            blk = zt[R_KA + hd * HEAD_DIM_A:R_KA + (hd + 1) * HEAD_DIM_A, :]
            ks.append(_rope_cols(blk * g_ka, cos_a, sin_a) * _rms_cols(blk, HEAD_DIM_A))
        ka_ref[tok, :] = jnp.concatenate(ks, axis=0).T.astype(BF16)
        vt = zt[R_VA:R_VA + A_KV, :].astype(BF16)
        for c in range(n // LANE):
            va_ref[off // LANE + c] = vt[:, c * LANE:(c + 1) * LANE]
        cq = zt[R_CQ:R_CQ + Q_LORA, :]
        cq_ref[sub] = (cq * g_ref[G_QL:G_QL + Q_LORA, :] * _rms_cols(cq, Q_LORA)).astype(BF16)
        ckv = zt[R_CKV:R_CKV + KV_LORA, :]
        ckv_ref[sub] = (ckv * g_ref[G_KVL:G_KVL + KV_LORA, :] * _rms_cols(ckv, KV_LORA)).astype(BF16)

    def up_project(sub):
        qbt_ref[sub] = jnp.dot(wuq_ref[...], cq_ref[sub], preferred_element_type=F32)
        kvt_ref[sub] = jnp.dot(wukv_ref[...], ckv_ref[sub], preferred_element_type=F32)

    def mla_heads(sub):
        off = sub * n
        tok = slice(off, off + n)
        _, _, cos_b, sin_b = tables(tok)
        g_mq = g_ref[G_MQ:G_MQ + QK_DIM_B, :]
        pad = jnp.zeros((QK_PAD_B - QK_DIM_B, n), BF16)
        for hd in range(N_HEADS_B):
            blk = qbt_ref[sub, hd * QK_PAD_B:hd * QK_PAD_B + QK_DIM_B, :]
            y = blk * g_mq
            o = jnp.concatenate([y[:NOPE_B], _rope_cols(y[NOPE_B:], cos_b, sin_b)], axis=0)
            o = (o * (_rms_cols(blk, QK_DIM_B) * MLA_Q_SCALE)).astype(BF16)
            step = min(n, MLA_TQ)
            for c in range(n // step):
                lo = off + c * step
                qb_ref[hd, lo // MLA_TQ, :, lo % MLA_TQ:lo % MLA_TQ + step] = (
                    jnp.concatenate([o[:, c * step:(c + 1) * step], pad[:, :step]], axis=0))
        kr = zt_ref[sub, R_KR:R_KR + ROPE_B, :]
        ss_kr = jnp.sum(kr * kr, axis=0, keepdims=True)
        kr_rot = _rope_cols(kr * g_ref[G_KR:G_KR + ROPE_B, :], cos_b, sin_b)
        g_kn = g_ref[G_KN:G_KN + NOPE_B, :]
        padf = jnp.zeros((QK_PAD_B - QK_DIM_B, n), F32)
        for hd in range(N_HEADS_B):
            kn = kvt_ref[sub, hd * 2 * NOPE_B:hd * 2 * NOPE_B + NOPE_B, :]
            r = lax.rsqrt((jnp.sum(kn * kn, axis=0, keepdims=True) + ss_kr) * (1.0 / QK_DIM_B) + EPS)
            kt = jnp.concatenate([kn * g_kn * r, kr_rot * r, padf], axis=0)
            kb_ref[hd, tok, :] = kt.T.astype(BF16)
            v = kvt_ref[sub, hd * 2 * NOPE_B + NOPE_B:(hd + 1) * 2 * NOPE_B, :].astype(BF16)
            vb_ref[hd, :, tok] = v

    assert PROJ_TM == 2 * PROJ_SUB
    project(0)
    project(1)
    swa_heads(0)
    up_project(0)
    swa_heads(1)
    mla_heads(0)
    up_project(1)
    mla_heads(1)


def _inproj(x, sh, sc, nw, w1t, wuqt, wukvt, gains, table):
    B, S, _ = x.shape
    tm = PROJ_TM
    nt = S // tm
    assert MLA_TK % tm == 0
    tiles_per_chunk = MLA_TK // tm
    out_shape = [
        jax.ShapeDtypeStruct((B, N_HEADS_A, HEAD_DIM_A, S), BF16),
        jax.ShapeDtypeStruct((B, S, A_KV), BF16),
        jax.ShapeDtypeStruct((B, S // LANE, A_KV, LANE), BF16),
        jax.ShapeDtypeStruct((B, N_HEADS_B, S // MLA_TQ, QK_PAD_B, MLA_TQ), BF16),
        jax.ShapeDtypeStruct((B, N_HEADS_B, S, QK_PAD_B), BF16),
        jax.ShapeDtypeStruct((B, N_HEADS_B, S // MLA_TK, V_DIM_B, MLA_TK), BF16),
    ]
    out_specs = [
        pl.BlockSpec((None, N_HEADS_A, HEAD_DIM_A, tm), lambda b, t: (b, 0, 0, t)),
        pl.BlockSpec((None, tm, A_KV), lambda b, t: (b, t, 0)),
        pl.BlockSpec((None, tm // LANE, A_KV, LANE), lambda b, t: (b, t, 0, 0)),
        pl.BlockSpec((None, N_HEADS_B, tm // MLA_TQ, QK_PAD_B, MLA_TQ), lambda b, t: (b, 0, t, 0, 0)),
        pl.BlockSpec((None, N_HEADS_B, tm, QK_PAD_B), lambda b, t: (b, 0, t, 0)),
        pl.BlockSpec((None, N_HEADS_B, None, V_DIM_B, tm), lambda b, t: (b, 0, t // tiles_per_chunk, 0, t % tiles_per_chunk)),
    ]
    return pl.pallas_call(
        _inproj_kernel,
        grid=(B, nt),
        in_specs=[
            pl.BlockSpec((None, tm, D_MODEL), lambda b, t: (b, t, 0)),
            _row_spec(), _row_spec(), _const_spec((1, D_MODEL)),
            _const_spec((IN_WIDTH, D_MODEL)),
            _const_spec((N_HEADS_B * QK_PAD_B, Q_LORA)),
            _const_spec((N_HEADS_B * 2 * NOPE_B, KV_LORA)),
            _const_spec((G_ROWS, PROJ_SUB)),
            pl.BlockSpec((T_ROWS, tm), lambda b, t: (0, t)),
        ],
        out_specs=out_specs,
        out_shape=out_shape,
        scratch_shapes=[
            pltpu.VMEM((2, IN_WIDTH, PROJ_SUB), F32),
            pltpu.VMEM((2, Q_LORA, PROJ_SUB), BF16),
            pltpu.VMEM((2, KV_LORA, PROJ_SUB), BF16),
            pltpu.VMEM((2, N_HEADS_B * QK_PAD_B, PROJ_SUB), F32),
            pltpu.VMEM((2, N_HEADS_B * 2 * NOPE_B, PROJ_SUB), F32),
        ],
        compiler_params=_cparams(("arbitrary", "arbitrary")),
        name="inproj",
    )(x, sh, sc, nw, w1t, wuqt, wukvt, gains, table)


def _swa_kernel(*refs, n_blocks, bounded):
    if bounded:
        shift_ref, q_ref, k_ref, v_ref, sink_ref, o_ref, p_ref = refs
    else:
        q_ref, k_ref, v_ref, sink_ref, o_ref, p_ref, m_ref = refs
    tq = q_ref.shape[-1]
    nq = tq // LANE
    wide = GROUP_A * LANE
    row = lax.broadcasted_iota(jnp.int32, (LANE, wide), 0)
    col = lax.broadcasted_iota(jnp.int32, (LANE, wide), 1) & (LANE - 1)
    diff = row - col
    zeros = jnp.zeros((HEAD_DIM_A, LANE), BF16)
    ones = jnp.ones((ONES_ROWS, LANE), BF16)
    chains = [(i, j) for i in range(nq) for j in range(N_KV_A)]

    def blocks(i):
        nb = pl.program_id(1) * nq + i
        kb_idx = [jnp.maximum(nb - 1, 0), nb, jnp.minimum(nb + 1, n_blocks - 1)]
        lo = jnp.where(nb > 0, 0, 2 * LANE)
        hi = jnp.where(nb < n_blocks - 1, 0, -2 * LANE)
        return kb_idx, [diff >= lo, None, diff <= hi]

    def probs(i, j, slot):
        kb_idx, masks = blocks(i)
        cols = []
        for g in range(GROUP_A):
            qh = q_ref[j * GROUP_A + g, :, i * LANE:(i + 1) * LANE]
            cols.append(jnp.concatenate([qh, zeros] if j == 0 else [zeros, qh], axis=0))
        qp = jnp.concatenate(cols, axis=1)
        k3 = jnp.concatenate([k_ref[pl.ds(pl.multiple_of(kb_idx[c] * LANE, LANE), LANE), :] for c in range(3)],
                             axis=0)
        s3 = jnp.dot(k3, qp, preferred_element_type=F32)
        ss = [s3[c * LANE:(c + 1) * LANE] for c in range(3)]
        if bounded:
            m = shift_ref[0, 0]
        else:
            m = sink_ref[j]
            for c in range(3):
                sc = ss[c] if masks[c] is None else jnp.where(masks[c], ss[c], NEG_BIG)
                m = jnp.maximum(m, jnp.max(sc, axis=0, keepdims=True))
            m_ref[slot] = m
        for c in range(3):
            p = jnp.exp2(ss[c] - m)
            if masks[c] is not None:
                p = jnp.where(masks[c], p, 0.0)
            p_ref[slot, c] = p.astype(BF16)

    def apply(i, j, slot):
        kb_idx, _ = blocks(i)
        m = shift_ref[0, 0] if bounded else m_ref[slot]
        vt = jnp.concatenate([v_ref[kb_idx[c], j * HEAD_DIM_A:(j + 1) * HEAD_DIM_A, :] for c in range(3)], axis=1)
        v1 = jnp.concatenate([vt, jnp.ones((ONES_ROWS, 3 * LANE), BF16)], axis=0)
        acc = jnp.dot(v1, p_ref[slot].reshape(3 * LANE, wide), preferred_element_type=F32)
        l = acc[HEAD_DIM_A:HEAD_DIM_A + 1] + jnp.exp2(sink_ref[j] - m)
        o = acc[:HEAD_DIM_A] * (1.0 / l)
        for g in range(GROUP_A):
            o_ref[j * GROUP_A + g, :, i * LANE:(i + 1) * LANE] = o[:, g * LANE:(g + 1) * LANE].astype(BF16)

    probs(*chains[0], 0)
    for c, (i, j) in enumerate(chains):
        if c + 1 < len(chains):
            probs(*chains[c + 1], (c + 1) % 2)
        apply(i, j, c % 2)


def _swa_call(bounded, qa_t, ka, va_t, sink_rows, *scalars):
    B, _, _, S = qa_t.shape
    tq = SWA_TQ
    nbk = S // LANE
    qspec = pl.BlockSpec((None, N_HEADS_A, HEAD_DIM_A, tq), lambda b, t: (b, 0, 0, t))
    return pl.pallas_call(
        functools.partial(_swa_kernel, n_blocks=nbk, bounded=bounded),
        grid=(B, S // tq),
        in_specs=[pl.BlockSpec(memory_space=pltpu.SMEM)] * len(scalars) + [
            qspec,
            pl.BlockSpec((None, S, A_KV), lambda b, t: (b, 0, 0)),
            pl.BlockSpec((None, nbk, A_KV, LANE), lambda b, t: (b, 0, 0, 0)),
            _const_spec((N_KV_A, 1, GROUP_A * LANE)),
        ],
        out_specs=qspec,
        out_shape=jax.ShapeDtypeStruct(qa_t.shape, BF16),
        scratch_shapes=[pltpu.VMEM((2, 3, LANE, GROUP_A * LANE), BF16)]
                       + ([] if bounded else [pltpu.VMEM((2, 1, GROUP_A * LANE), F32)]),
        compiler_params=_cparams(("arbitrary", "arbitrary")),
        name="swa_bounded" if bounded else "swa_online",
    )(*scalars, qa_t, ka, va_t, sink_rows)


def _swa(qa_t, ka, va_t, sink_rows, score_bound):
    shift = jnp.maximum(score_bound, jnp.max(sink_rows))
    return lax.cond(
        shift + score_bound <= SOFTMAX_MAX_SHIFT_RANGE,
        lambda: _swa_call(True, qa_t, ka, va_t, sink_rows, shift.reshape(1, 1).astype(F32)),
        lambda: _swa_call(False, qa_t, ka, va_t, sink_rows),
    )


def _mla_online_kernel(q_ref, k_ref, v_ref, o_ref, m_ref, acc_ref):
    nq = q_ref.shape[0]
    n = v_ref.shape[0]
    tk = v_ref.shape[-1]
    ones = jnp.ones((ONES_ROWS, tk), BF16)

    def qblock(qb, carry):
        m_ref[...] = jnp.full_like(m_ref, NEG_BIG)
        acc_ref[...] = jnp.zeros_like(acc_ref)

        def chunk(c, carry2):
            k = k_ref[pl.ds(pl.multiple_of(c * tk, tk), tk), :]
            s = jnp.dot(k, q_ref[qb], preferred_element_type=F32)
            m_old = m_ref[...]
            m_new = jnp.maximum(m_old, jnp.max(s, axis=0, keepdims=True))
            p = jnp.exp2(s - m_new).astype(BF16)
            v1 = jnp.concatenate([v_ref[c], ones], axis=0)
            acc_ref[...] = jnp.exp2(m_old - m_new) * acc_ref[...] + jnp.dot(v1, p, preferred_element_type=F32)
            m_ref[...] = m_new
            return carry2

        lax.fori_loop(0, n, chunk, 0)
        acc = acc_ref[...]
        o_ref[qb] = (acc[:V_DIM_B] * (1.0 / acc[V_DIM_B:V_DIM_B + 1])).astype(BF16)
        return carry

    lax.fori_loop(0, nq, qblock, 0)


def _mla_bounded_kernel(shift_ref, q_ref, k_ref, v_ref, o_ref, p_ref):
    nq = q_ref.shape[0]
    n = v_ref.shape[0]
    tk = v_ref.shape[-1]
    total = nq * n
    log2n = n.bit_length() - 1
    qpb = min(nq, max(1, MLA_MAX_UNROLL // n))
    shift = shift_ref[0, 0]
    ones = jnp.ones((PV_LHS_ROWS - V_DIM_B, tk), BF16)

    def probs(t, slot):
        c = t & (n - 1)
        k = k_ref[pl.ds(pl.multiple_of(c * tk, tk), tk), :]
        s = jnp.dot(k, q_ref[t >> log2n], preferred_element_type=F32)
        p_ref[slot] = jnp.exp2(s - shift).astype(BF16)

    def pv(c, slot):
        v1 = jnp.concatenate([v_ref[c], ones], axis=0)
        return jnp.dot(v1, p_ref[slot], preferred_element_type=F32)

    probs(0, 0)

    def body(j, carry):
        t0 = j * (qpb * n)
        for u in range(qpb * n):
            probs(jnp.minimum(t0 + u + 1, total - 1), (u + 1) % 2)
            part = pv(u % n, u % 2)
            acc = part if u % n == 0 else acc + part
            if u % n == n - 1:
                o_ref[j * qpb + u // n] = (acc[:V_DIM_B] * (1.0 / acc[V_DIM_B:V_DIM_B + 1])).astype(BF16)
        return carry

    lax.fori_loop(0, nq // qpb, body, 0)


def _mla_call(kernel_fn, scratch, qb_t, kb, vb_t, *scalars):
    B, H, nq, _, tq = qb_t.shape
    nck, tk = vb_t.shape[2], vb_t.shape[4]
    S = nck * tk
    assert nck & (nck - 1) == 0 and nq & (nq - 1) == 0
    return pl.pallas_call(
        kernel_fn,
        grid=(B, H),
        in_specs=[pl.BlockSpec(memory_space=pltpu.SMEM)] * len(scalars) + [
            pl.BlockSpec((None, None, nq, QK_PAD_B, tq), lambda b, h: (b, h, 0, 0, 0)),
            pl.BlockSpec((None, None, S, QK_PAD_B), lambda b, h: (b, h, 0, 0)),
            pl.BlockSpec((None, None, nck, V_DIM_B, tk), lambda b, h: (b, h, 0, 0, 0)),
        ],
        out_specs=pl.BlockSpec((None, None, nq, V_DIM_B, tq), lambda b, h: (b, h, 0, 0, 0)),
        out_shape=jax.ShapeDtypeStruct((B, H, nq, V_DIM_B, tq), BF16),
        scratch_shapes=scratch(tk, tq),
        compiler_params=_cparams(("arbitrary", "arbitrary")),
        name=kernel_fn.__name__.strip("_").replace("_kernel", ""),
    )(*scalars, qb_t, kb, vb_t)


def _mla_online_scratch(tk, tq):
    return [pltpu.VMEM((1, tq), F32), pltpu.VMEM((V_DIM_B + ONES_ROWS, tq), F32)]


def _mla_bounded_scratch(tk, tq):
    return [pltpu.VMEM((2, tk, tq), BF16)]


def _mla(qb_t, kb, vb_t, score_bound):
    shift = score_bound.reshape(1, 1).astype(F32)
    return lax.cond(
        2.0 * score_bound <= SOFTMAX_MAX_SHIFT_RANGE,
        lambda: _mla_call(_mla_bounded_kernel, _mla_bounded_scratch, qb_t, kb, vb_t, shift),
        lambda: _mla_call(_mla_online_kernel, _mla_online_scratch, qb_t, kb, vb_t),
    )


def _rope_table(S):
    pos = jnp.arange(S, dtype=F32)

    def cs(d):
        inv = ROPE_THETA ** (-jnp.arange(0, d, 2, dtype=F32) / d)
        ang = pos[:, None] * inv[None, :]
        return jnp.cos(ang).T, jnp.sin(ang).T

    ca, sa = cs(HEAD_DIM_A)
    cb, sb = cs(ROPE_B)
    return jnp.concatenate([ca, sa, cb, sb], axis=0)


def _gain_tile(swa_q_norm, swa_k_norm, mla_q_lora_norm, mla_kv_lora_norm, mla_q_norm, mla_k_norm):
    rows = jnp.concatenate([
        swa_q_norm, swa_k_norm, mla_q_lora_norm, mla_kv_lora_norm,
        mla_q_norm, mla_k_norm[:NOPE_B], mla_k_norm[NOPE_B:],
    ]).astype(F32)
    return jnp.broadcast_to(rows[:, None], (G_ROWS, PROJ_SUB))


def _score_bound(dim, q_scale, q_gain, k_gain):
    return (BF16_ROUND_MARGIN * dim * q_scale * jnp.max(jnp.abs(q_gain)) * jnp.max(jnp.abs(k_gain))).astype(F32)


def kernel(x_prompt, x_sample, c_prompt, c_sample, ada_w, ada_b, ffn1_norm, ffn1_wg, ffn1_wu, ffn1_wd, mix_norm, w_in, swa_q_norm, swa_k_norm, swa_sink, mla_q_lora_norm, mla_w_uq, mla_kv_lora_norm, mla_w_ukv, mla_q_norm, mla_k_norm, w_out, ffn2_norm, ffn2_wg, ffn2_wu, ffn2_wd, final_norm):
    assert ada_w.shape[0] == 1, "single-layer trunk"
    bp, bs = x_prompt.shape[0], x_sample.shape[0]
    assert bp + bs <= MOD_ROWS

    f1 = (ffn1_wg[0].astype(BF16), ffn1_wu[0].astype(BF16), ffn1_wd[0].astype(BF16))
    f2 = (ffn2_wg[0].astype(BF16), ffn2_wu[0].astype(BF16), ffn2_wd[0].astype(BF16))
    w1t = w_in[0].T.astype(BF16)
    wuq = mla_w_uq[0].reshape(Q_LORA, N_HEADS_B, QK_DIM_B)
    wuq = jnp.pad(wuq, ((0, 0), (0, 0), (0, QK_PAD_B - QK_DIM_B))).reshape(Q_LORA, N_HEADS_B * QK_PAD_B)
    wuqt = wuq.T.astype(BF16)
    wukvt = mla_w_ukv[0].T.astype(BF16)
    woa = w_out[0, :A_Q].astype(BF16)
    wob = w_out[0, A_Q:].astype(BF16)
    gains = _gain_tile(swa_q_norm[0], swa_k_norm[0], mla_q_lora_norm[0], mla_kv_lora_norm[0],
                       mla_q_norm[0], mla_k_norm[0])
    sink_rows = jnp.repeat(swa_sink[0].astype(F32) * LOG2E, LANE).reshape(N_KV_A, 1, GROUP_A * LANE)
    row = lambda v: v.reshape(1, D_MODEL)
    bound_a = _score_bound(HEAD_DIM_A, SWA_Q_SCALE, swa_q_norm[0], swa_k_norm[0])
    bound_b = _score_bound(QK_DIM_B, MLA_Q_SCALE, mla_q_norm[0], mla_k_norm[0])

    table = _rope_table(max(x_prompt.shape[1], x_sample.shape[1]))
    c_all = jnp.concatenate([c_prompt, c_sample, jnp.zeros((MOD_ROWS - bp - bs, D_MODEL), F32)], axis=0)
    mod = _modulation(c_all, ada_w[0], ada_b[0].reshape(1, -1)).reshape(MOD_ROWS, N_MOD, 1, D_MODEL)

    def trunk(x, m):
        sh1, sc1, g1, sh2, sc2, g2, sh3, sc3, g3 = (m[:, i] for i in range(N_MOD))
        S = x.shape[1]
        x1 = _ffn1(x, sh1, sc1, g1, row(ffn1_norm[0]), *f1)
        qa_t, ka, va_t, qb_t, kb, vb_t = _inproj(x1, sh2, sc2, row(mix_norm[0]), w1t, wuqt, wukvt,
                                                  gains, table)
        oa_t = _swa(qa_t, ka, va_t, sink_rows, bound_a)
        ob_t = _mla(qb_t, kb, vb_t, bound_b)
        return _ffn2(x1, oa_t.reshape(-1, A_Q, S), ob_t, woa, wob, g2, sh3, sc3, g3,
                     row(ffn2_norm[0]), *f2, row(final_norm[0]))

    y_prompt = trunk(x_prompt, mod[:bp])
    y_sample = trunk(x_sample, mod[bp:bp + bs])
    return (y_prompt, y_sample)
```
